```python
import math
import jax, jax.numpy as jnp
from jax import lax
import numpy as np

D_MODEL = 4096
BATCH = 4
SEQ = 4096
DEPTH = 1

M_HEADS = 8
M_QK_DIM = D_MODEL // (2 * M_HEADS)
M_V_DIM = D_MODEL // M_HEADS
M_QK = M_HEADS * M_QK_DIM
M_V = M_HEADS * M_V_DIM
M_CHUNK = 64
CONV_WIDTH = 3

A_HEADS = D_MODEL // 128
A_NOPE = 128
A_ROPE = 64
A_V = 128
Q_LORA = D_MODEL // 4
KV_LORA = 512
ROPE_THETA = 10000.0
Q_BLOCK = 128
ATTN_SCALE = (A_NOPE + A_ROPE) ** -0.5

N_GROUPS = 8
EXPERTS_PER_GROUP = 4
N_EXPERTS = N_GROUPS * EXPERTS_PER_GROUP
TOP_K = 2
D_EXPERT = D_MODEL // 4

LN_EPS = 1e-5
RMS_EPS = 1e-6
DEEPNORM_ALPHA = (2 * DEPTH) ** 0.25
DEEPNORM_BETA = (8 * DEPTH) ** -0.25

IN_SIZES = (M_QK, M_QK, M_V, M_V, 4 * M_HEADS, Q_LORA, KV_LORA, A_ROPE, D_MODEL, D_MODEL)

kernel_name = "hybrid_mlstm_mla_hmoe_deepnorm_encoder"


def layer_norm(x, g, b):
    xf = x.astype(jnp.float32)
    mu = jnp.mean(xf, -1, keepdims=True)
    xc = xf - mu
    var = jnp.mean(xc * xc, -1, keepdims=True)
    return (xc * lax.rsqrt(var + LN_EPS) * g.astype(jnp.float32) + b.astype(jnp.float32)).astype(x.dtype)


def rms_norm(x, g):
    xf = x.astype(jnp.float32)
    return (xf * lax.rsqrt(jnp.mean(xf * xf, -1, keepdims=True) + RMS_EPS) * g.astype(jnp.float32)).astype(x.dtype)


def rope_tables(seq, dim):
    inv = 1.0 / (ROPE_THETA ** (jnp.arange(0, dim, 2, dtype=jnp.float32) / dim))
    ang = jnp.arange(seq, dtype=jnp.float32)[:, None] * inv[None, :]
    return jnp.cos(ang), jnp.sin(ang)


def apply_rope(x, cos, sin):
    xf = x.astype(jnp.float32)
    x1, x2 = jnp.split(xf, 2, axis=-1)
    return jnp.concatenate([x1 * cos - x2 * sin, x1 * sin + x2 * cos], -1).astype(x.dtype)


def centred_depthwise_conv(x, w):
    pad = CONV_WIDTH // 2
    return lax.conv_general_dilated(
        x, w[:, None, :].astype(x.dtype), window_strides=(1,), padding=[(pad, pad)],
        dimension_numbers=('NWC', 'WIO', 'NWC'), feature_group_count=x.shape[-1])


def mlstm_chunkwise(q, k, v, log_i, log_f):
    B, H, S, dk = q.shape
    dv = v.shape[-1]
    L = M_CHUNK
    nc = S // L

    def to_chunks(t):
        return jnp.moveaxis(t.reshape((B, H, nc, L) + t.shape[3:]), 2, 0)

    xs = tuple(to_chunks(t) for t in (q, k, v, log_i, log_f))
    lower = jnp.tril(jnp.ones((L, L), dtype=bool))

    def step(carry, blk):
        C, n, m = carry
        qb, kb, vb, ib, fb = blk
        b = jnp.cumsum(fb, axis=-1)
        dmat = b[..., :, None] - b[..., None, :] + ib[..., None, :]
        dmat = jnp.where(lower, dmat, -jnp.inf)
        inter = b + m[..., None]
        m_t = jnp.maximum(inter, jnp.max(dmat, -1))
        w_inter = jnp.exp(inter - m_t)
        s_qk = jnp.einsum('bhtd,bhsd->bhts', qb, kb) * jnp.exp(dmat - m_t[..., None])
        num = w_inter[..., None] * jnp.einsum('bhtd,bhde->bhte', qb, C) + jnp.einsum('bhts,bhse->bhte', s_qk, vb)
        den = w_inter * jnp.einsum('bhtd,bhd->bht', qb, n) + jnp.sum(s_qk, -1)
        h = num / jnp.maximum(jnp.abs(den), jnp.exp(-m_t))[..., None]
        b_last = b[..., -1]
        dec = b_last[..., None] - b + ib
        m_new = jnp.maximum(b_last + m, jnp.max(dec, -1))
        a = jnp.exp(b_last + m - m_new)
        ws = jnp.exp(dec - m_new[..., None])
        C_new = a[..., None, None] * C + jnp.einsum('bhs,bhsd,bhse->bhde', ws, kb, vb)
        n_new = a[..., None] * n + jnp.einsum('bhs,bhsd->bhd', ws, kb)
        return (C_new, n_new, m_new), h

    init = (jnp.zeros((B, H, dk, dv), jnp.float32), jnp.zeros((B, H, dk), jnp.float32),
            jnp.zeros((B, H), jnp.float32))
    _, hc = lax.scan(step, init, xs)
    return jnp.moveaxis(hc, 0, 2).reshape(B, H, S, dv)


def mlstm_branch(q, k, v, o, gates, b_gates, conv_w, head_norm_g):
    B, S, _ = q.shape
    qk = jax.nn.silu(centred_depthwise_conv(jnp.concatenate([q, k], -1), conv_w))
    q, k = jnp.split(qk, 2, axis=-1)

    def heads(t, d):
        return t.reshape(B, S, M_HEADS, d).transpose(0, 2, 1, 3).astype(jnp.float32)

    qh = heads(q, M_QK_DIM)
    kh = heads(k, M_QK_DIM) * (M_QK_DIM ** -0.5)
    vh = heads(v, M_V_DIM)
    g = (gates.astype(jnp.float32) + b_gates.astype(jnp.float32)).reshape(B, S, 4, M_HEADS).transpose(2, 0, 3, 1)
    h_fwd = mlstm_chunkwise(qh, kh, vh, g[0], jax.nn.log_sigmoid(g[1]))
    flip = lambda t: jnp.flip(t, axis=2)
    h_bwd = flip(mlstm_chunkwise(flip(qh), flip(kh), flip(vh), flip(g[2]), flip(jax.nn.log_sigmoid(g[3]))))
    h = h_fwd + h_bwd
    h = h * lax.rsqrt(jnp.mean(h * h, -1, keepdims=True) + RMS_EPS)
    h = h.transpose(0, 2, 1, 3).reshape(B, S, M_V) * head_norm_g.astype(jnp.float32)
    return (jax.nn.sigmoid(o.astype(jnp.float32)) * h).astype(o.dtype)


def mla_branch(c_q, c_kv, k_r, q_norm_g, w_uq, kv_norm_g, w_ukv, cos, sin):
    B, S, _ = c_q.shape
    q = (rms_norm(c_q, q_norm_g) @ w_uq).reshape(B, S, A_HEADS, A_NOPE + A_ROPE).transpose(0, 2, 1, 3)
    q_nope = q[..., :A_NOPE]
    q_rope = apply_rope(q[..., A_NOPE:], cos, sin)
    kv = (rms_norm(c_kv, kv_norm_g) @ w_ukv).reshape(B, S, A_HEADS, A_NOPE + A_V).transpose(0, 2, 1, 3)
    k_nope, v = kv[..., :A_NOPE], kv[..., A_NOPE:]
    k_rope = apply_rope(k_r, cos, sin)
    n_blocks = S // Q_BLOCK

    def to_blocks(t):
        return t.reshape(B, A_HEADS, n_blocks, Q_BLOCK, t.shape[-1]).transpose(2, 0, 1, 3, 4)

    def attend(blk):
        qn, qr = blk
        s = (jnp.einsum('bhqd,bhkd->bhqk', qn, k_nope).astype(jnp.float32)
             + jnp.einsum('bhqd,bkd->bhqk', qr, k_rope).astype(jnp.float32)) * ATTN_SCALE
        p = jax.nn.softmax(s, axis=-1).astype(v.dtype)
        return jnp.einsum('bhqk,bhkd->bhqd', p, v)

    o = lax.map(attend, (to_blocks(q_nope), to_blocks(q_rope)))
    return o.transpose(1, 0, 3, 2, 4).reshape(B, S, A_HEADS * A_V)


def hier_moe(x, w_rg, b_rg, w_re, b_re, w_gate, w_up, w_down):
    B, S, D = x.shape
    T = B * S
    xt = x.reshape(T, D)
    g_logits = (xt @ w_rg).astype(jnp.float32) + b_rg.astype(jnp.float32)
    g_prob = jax.nn.softmax(g_logits, axis=-1)
    g_top, g_idx = lax.top_k(g_prob, 1)
    e_logits = ((xt @ w_re).astype(jnp.float32) + b_re.astype(jnp.float32)).reshape(T, N_GROUPS, EXPERTS_PER_GROUP)
    e_in = jnp.take_along_axis(e_logits, g_idx[:, :, None], axis=1)[:, 0]
    top_v, top_i = lax.top_k(e_in, TOP_K)
    top_w = jax.nn.softmax(top_v, axis=-1) * g_top
    eid = g_idx * EXPERTS_PER_GROUP + top_i
    combine = jnp.sum(jax.nn.one_hot(eid, N_EXPERTS, dtype=jnp.float32) * top_w[..., None], axis=1)

    def expert(acc, e):
        wg, wu, wd, c = e
        h = jax.nn.silu(xt @ wg) * (xt @ wu)
        return acc + c[:, None].astype(xt.dtype) * (h @ wd), None

    out, _ = lax.scan(expert, jnp.zeros_like(xt), (w_gate, w_up, w_down, combine.T))
    return out.reshape(B, S, D)


def setup_inputs(seed: int = 0) -> dict:
    key = jax.random.key(seed)
    ks = iter(jax.random.split(key, 64))
    f32 = jnp.float32
    Lr = DEPTH
    beta = DEEPNORM_BETA

    def nrm(shape, scale):
        return jax.random.normal(next(ks), shape, f32) * scale

    def gain(shape):
        return 1.0 + nrm(shape, 0.02)

    x = nrm((BATCH, SEQ, D_MODEL), 1.0)
    ln0_g = gain((D_MODEL,))
    ln0_b = nrm((D_MODEL,), 0.02)
    in_scales = (1.0, 1.0, beta, 1.0, 1.0, 1.0, 1.0, 1.0, 1.0, 1.0)
    w_in = jnp.concatenate([nrm((Lr, D_MODEL, n), D_MODEL ** -0.5 * s) for n, s in zip(IN_SIZES, in_scales)], axis=-1)
    i_bias = nrm((Lr, M_HEADS), 0.1)
    f_bias = jnp.linspace(3.0, 6.0, M_HEADS, dtype=f32) + nrm((Lr, M_HEADS), 0.1)
    i_bias_b = nrm((Lr, M_HEADS), 0.1)
    f_bias_b = jnp.linspace(3.0, 6.0, M_HEADS, dtype=f32) + nrm((Lr, M_HEADS), 0.1)
    b_gates = jnp.concatenate([i_bias, f_bias, i_bias_b, f_bias_b], axis=-1)
    conv_qk = nrm((Lr, CONV_WIDTH, 2 * M_QK), CONV_WIDTH ** -0.5)
    head_norm_g = gain((Lr, M_V))
    q_norm_g = gain((Lr, Q_LORA))
    w_uq = nrm((Lr, Q_LORA, A_HEADS * (A_NOPE + A_ROPE)), Q_LORA ** -0.5)
    kv_norm_g = gain((Lr, KV_LORA))
    w_uk = nrm((Lr, KV_LORA, A_HEADS, A_NOPE), KV_LORA ** -0.5)
    w_uv = nrm((Lr, KV_LORA, A_HEADS, A_V), KV_LORA ** -0.5 * beta)
    w_ukv = jnp.concatenate([w_uk, w_uv], axis=-1).reshape(Lr, KV_LORA, A_HEADS * (A_NOPE + A_V))
    w_pm = nrm((Lr, M_V, D_MODEL), M_V ** -0.5 * beta)
    w_pa = nrm((Lr, A_HEADS * A_V, D_MODEL), (A_HEADS * A_V) ** -0.5 * beta)
    w_out = nrm((Lr, D_MODEL, D_MODEL), D_MODEL ** -0.5 * beta)
    ln1_g = gain((Lr, D_MODEL))
    ln1_b = nrm((Lr, D_MODEL), 0.02)
    w_rg = nrm((Lr, D_MODEL, N_GROUPS), D_MODEL ** -0.5)
    b_rg = nrm((Lr, N_GROUPS), 0.01)
    w_re = nrm((Lr, D_MODEL, N_EXPERTS), D_MODEL ** -0.5)
    b_re = nrm((Lr, N_EXPERTS), 0.01)
    w_e_gate = nrm((Lr, N_EXPERTS, D_MODEL, D_EXPERT), D_MODEL ** -0.5)
    w_e_up = nrm((Lr, N_EXPERTS, D_MODEL, D_EXPERT), D_MODEL ** -0.5 * beta)
    w_e_down = nrm((Lr, N_EXPERTS, D_EXPERT, D_MODEL), D_EXPERT ** -0.5 * beta)
    ln2_g = gain((Lr, D_MODEL))
    ln2_b = nrm((Lr, D_MODEL), 0.02)
    return {"x": x, "ln0_g": ln0_g, "ln0_b": ln0_b, "w_in": w_in, "b_gates": b_gates,
            "conv_qk": conv_qk, "head_norm_g": head_norm_g, "q_norm_g": q_norm_g, "w_uq": w_uq,
            "kv_norm_g": kv_norm_g, "w_ukv": w_ukv, "w_pm": w_pm, "w_pa": w_pa, "w_out": w_out,
            "ln1_g": ln1_g, "ln1_b": ln1_b, "w_rg": w_rg, "b_rg": b_rg, "w_re": w_re, "b_re": b_re,
            "w_e_gate": w_e_gate, "w_e_up": w_e_up, "w_e_down": w_e_down, "ln2_g": ln2_g, "ln2_b": ln2_b}


def reference(x, ln0_g, ln0_b, w_in, b_gates, conv_qk, head_norm_g, q_norm_g, w_uq, kv_norm_g, w_ukv,
              w_pm, w_pa, w_out, ln1_g, ln1_b, w_rg, b_rg, w_re, b_re, w_e_gate, w_e_up, w_e_down,
              ln2_g, ln2_b):
    B, S, _ = x.shape
    split_points = [int(p) for p in np.cumsum(IN_SIZES)[:-1]]
    cos, sin = rope_tables(S, A_ROPE)
    x = layer_norm(x, ln0_g, ln0_b)
    for l in range(DEPTH):
        proj = x @ w_in[l]
        q_m, k_m, v_m, o_m, g_m, c_q, c_kv, k_r, gate_a, gate_b = jnp.split(proj, split_points, axis=-1)
        y_a = mlstm_branch(q_m, k_m, v_m, o_m, g_m, b_gates[l], conv_qk[l], head_norm_g[l]) @ w_pm[l]
        y_b = mla_branch(c_q, c_kv, k_r, q_norm_g[l], w_uq[l], kv_norm_g[l], w_ukv[l], cos, sin) @ w_pa[l]
        mixed = (jax.nn.sigmoid(gate_a) * y_a + jax.nn.sigmoid(gate_b) * y_b) @ w_out[l]
        x = layer_norm(DEEPNORM_ALPHA * x + mixed, ln1_g[l], ln1_b[l])
        moe = hier_moe(x, w_rg[l], b_rg[l], w_re[l], b_re[l], w_e_gate[l], w_e_up[l], w_e_down[l])
        x = layer_norm(DEEPNORM_ALPHA * x + moe, ln2_g[l], ln2_b[l])
    return x
```

```python
import functools

import jax
import jax.numpy as jnp
import numpy as np
from jax import lax
from jax.experimental import pallas as pl
from jax.experimental.pallas import tpu as pltpu

D_MODEL = 4096
M_HEADS = 8
M_QK_DIM = 256
M_V_DIM = 512
M_QK = M_HEADS * M_QK_DIM
M_V = M_HEADS * M_V_DIM
A_HEADS = 32
A_NOPE = 128
A_ROPE = 64
A_V = 128
Q_LORA = 1024
KV_LORA = 512
ROPE_THETA = 10000.0
ATTN_SCALE = (A_NOPE + A_ROPE) ** -0.5
N_GROUPS = 8
EXPERTS_PER_GROUP = 4
N_EXPERTS = 32
TOP_K = 2
D_EXPERT = 1024
LN_EPS = 1e-5
RMS_EPS = 1e-6
DEPTH = 1
DEEPNORM_ALPHA = (2 * DEPTH) ** 0.25
IN_SIZES = (M_QK, M_QK, M_V, M_V, 4 * M_HEADS, Q_LORA, KV_LORA, A_ROPE, D_MODEL, D_MODEL)

V7X_LANES = 128
V7X_VMEM_LIMIT = 56 * 1024 * 1024

MLSTM_CHUNK = 256
MOE_TILE = 512
SMALL_W = 2048
SMALL_CKV = Q_LORA
SMALL_KR = Q_LORA + KV_LORA
SMALL_GATES = SMALL_KR + 2 * A_ROPE
COL_QK = 0
COL_V = 2 * M_QK
COL_O = COL_V + M_V
COL_GA = COL_O + M_V
COL_GB = COL_GA + D_MODEL
COL_SMALL = COL_GB + D_MODEL
ROUTER_W = 128


def _params(*sem):
    return pltpu.CompilerParams(dimension_semantics=sem, vmem_limit_bytes=V7X_VMEM_LIMIT)


def _tile(n, pref):
    t = min(n, pref)
    while n % t:
        t //= 2
    return t


def _sigmoid(x):
    return 1.0 / (1.0 + jnp.exp(-x))


def _ln_body(z, g_ref, b_ref):
    mu = jnp.mean(z, axis=-1, keepdims=True)
    zc = z - mu
    var = jnp.mean(zc * zc, axis=-1, keepdims=True)
    return zc * lax.rsqrt(var + LN_EPS) * g_ref[...] + b_ref[...]


def _ln_kernel(x_ref, g_ref, b_ref, of_ref, ob_ref):
    y = _ln_body(x_ref[...], g_ref, b_ref)
    of_ref[...] = y
    ob_ref[...] = y.astype(jnp.bfloat16)


def layer_norm_in(x, g, b):
    T, D = x.shape
    tm = _tile(T, 256)
    row = pl.BlockSpec((tm, D), lambda i: (i, 0))
    vec = pl.BlockSpec((1, D), lambda i: (0, 0))
    return pl.pallas_call(
        _ln_kernel, grid=(T // tm,), in_specs=[row, vec, vec], out_specs=[row, row],
        out_shape=[jax.ShapeDtypeStruct((T, D), jnp.float32), jax.ShapeDtypeStruct((T, D), jnp.bfloat16)],
        compiler_params=_params("parallel"), name="ln_in",
    )(x, g.reshape(1, D), b.reshape(1, D))


def _mm_kernel(a_ref, w_ref, o_ref):
    o_ref[...] = jnp.dot(a_ref[...], w_ref[...], preferred_element_type=jnp.float32).astype(o_ref.dtype)


def matmul(a, w, out_dtype, col0=0, ncols=None, tm=1024, tn=1024, name="mm"):
    M, K = a.shape
    ncols = w.shape[1] - col0 if ncols is None else ncols
    tm = _tile(M, tm)
    tn = _tile(ncols, tn)
    assert col0 % tn == 0
    cb = col0 // tn
    return pl.pallas_call(
        _mm_kernel, grid=(ncols // tn, M // tm),
        in_specs=[pl.BlockSpec((tm, K), lambda j, i: (i, 0)),
                  pl.BlockSpec((K, tn), lambda j, i: (0, j + cb))],
        out_specs=pl.BlockSpec((tm, tn), lambda j, i: (i, j)),
        out_shape=jax.ShapeDtypeStruct((M, ncols), out_dtype),
        compiler_params=_params("parallel", "parallel"), name=name,
    )(a, w)


def _conv_silu_kernel(seq, x_ref, prev_ref, next_ref, w_ref, s_ref, o_ref):
    tm = x_ref.shape[0]
    i = pl.program_id(0)
    x = x_ref[...]
    row = lax.broadcasted_iota(jnp.int32, x.shape, 0)
    first = (i * tm) % seq == 0
    last = ((i + 1) * tm) % seq == 0
    prev_row = jnp.where(first, 0.0, prev_ref[7:8, :])
    next_row = jnp.where(last, 0.0, next_ref[0:1, :])
    x_prev = jnp.where(row == 0, prev_row, pltpu.roll(x, 1, 0))
    x_next = jnp.where(row == tm - 1, next_row, pltpu.roll(x, tm - 1, 0))
    y = w_ref[0:1, :] * x_prev + w_ref[1:2, :] * x + w_ref[2:3, :] * x_next
    y = y * _sigmoid(y) * s_ref[...]
    o_ref[...] = y.astype(o_ref.dtype)


def conv_silu(qk, conv_w, col_scale, seq):
    T, C = qk.shape
    tm = _tile(seq, 512)
    tc = _tile(C, 1024)
    nb8 = T // 8
    return pl.pallas_call(
        functools.partial(_conv_silu_kernel, seq), grid=(T // tm, C // tc),
        in_specs=[pl.BlockSpec((tm, tc), lambda i, j: (i, j)),
                  pl.BlockSpec((8, tc), lambda i, j: (jnp.maximum(i * (tm // 8) - 1, 0), j)),
                  pl.BlockSpec((8, tc), lambda i, j: (jnp.minimum((i + 1) * (tm // 8), nb8 - 1), j)),
                  pl.BlockSpec((3, tc), lambda i, j: (0, j)),
                  pl.BlockSpec((1, tc), lambda i, j: (0, j))],
        out_specs=pl.BlockSpec((tm, tc), lambda i, j: (i, j)),
        out_shape=jax.ShapeDtypeStruct((T, C), jnp.bfloat16),
        compiler_params=_params("parallel", "parallel"), name="conv_silu",
    )(qk, qk, qk, conv_w, col_scale)


def _log_sigmoid(x):
    return jnp.minimum(x, 0.0) - jnp.log(1.0 + jnp.exp(-jnp.abs(x)))


def _mlstm_chunk(q, k, v, gi_row, gf_row, c_ref, n_ref, m_ref, reverse):
    L = q.shape[0]
    t_idx = lax.broadcasted_iota(jnp.int32, (L, L), 0)
    s_idx = lax.broadcasted_iota(jnp.int32, (L, L), 1)
    seen = (s_idx >= t_idx) if reverse else (s_idx <= t_idx)
    eye = s_idx == t_idx
    f_row = _log_sigmoid(gf_row)
    b_col = jnp.sum(jnp.where(seen, f_row, 0.0), axis=1, keepdims=True)
    b_row = jnp.sum(jnp.where(eye, b_col, 0.0), axis=0, keepdims=True)
    i_col = jnp.sum(jnp.where(eye, gi_row, 0.0), axis=1, keepdims=True)
    b_last = jnp.sum(f_row, axis=1, keepdims=True)
    m_prev = m_ref[...]

    dmat = jnp.where(seen, b_col - b_row + gi_row, -jnp.inf)
    inter = b_col + m_prev
    m_t = jnp.maximum(inter, jnp.max(dmat, axis=1, keepdims=True))
    w_inter = jnp.exp(inter - m_t)
    s_qk = lax.dot_general(q, k, (((1,), (1,)), ((), ())), preferred_element_type=jnp.float32)
    s_qk = s_qk * jnp.exp(dmat - m_t)
    c_prev = c_ref[...]
    num = w_inter * jnp.dot(q, c_prev.astype(jnp.bfloat16), preferred_element_type=jnp.float32)
    num = num + jnp.dot(s_qk.astype(jnp.bfloat16), v, preferred_element_type=jnp.float32)
    qn = jnp.sum(q.astype(jnp.float32) * n_ref[...], axis=1, keepdims=True)
    den = w_inter * qn + jnp.sum(s_qk, axis=1, keepdims=True)
    h = num / jnp.maximum(jnp.abs(den), jnp.exp(-m_t))

    dec_col = b_last - b_col + i_col
    m_new = jnp.maximum(b_last + m_prev, jnp.max(dec_col, axis=0, keepdims=True))
    a = jnp.exp(b_last + m_prev - m_new)
    kw = k.astype(jnp.float32) * jnp.exp(dec_col - m_new)
    c_ref[...] = a * c_prev + lax.dot_general(
        kw.astype(jnp.bfloat16), v, (((0,), (0,)), ((), ())), preferred_element_type=jnp.float32)
    n_ref[...] = a * n_ref[...] + jnp.sum(kw, axis=0, keepdims=True)
    m_ref[...] = m_new
    return h


def _mlstm_kernel(q_ref, k_ref, v_ref, g_ref, bias_ref, o_ref, hg_ref, out_ref, hf_ref, c_ref, n_ref, m_ref):
    L = MLSTM_CHUNK
    nc = q_ref.shape[0] // L

    def reset():
        c_ref[...] = jnp.zeros_like(c_ref)
        n_ref[...] = jnp.zeros_like(n_ref)
        m_ref[...] = jnp.zeros_like(m_ref)

    def chunk(c, reverse):
        r0 = pl.multiple_of(c * L, L)
        rows = pl.ds(r0, L)
        g = g_ref[0, 0, c] + bias_ref[0]
        gi, gf = (g[2:3], g[3:4]) if reverse else (g[0:1], g[1:2])
        h = _mlstm_chunk(q_ref[rows, :], k_ref[rows, :], v_ref[rows, :], gi, gf, c_ref, n_ref, m_ref, reverse)
        return rows, h

    reset()

    def fwd(c, carry):
        rows, h = chunk(c, False)
        hf_ref[rows, :] = h
        return carry

    lax.fori_loop(0, nc, fwd, 0)
    reset()

    def bwd(j, carry):
        rows, h = chunk(nc - 1 - j, True)
        h = h + hf_ref[rows, :]
        h = h * lax.rsqrt(jnp.mean(h * h, axis=-1, keepdims=True) + RMS_EPS) * hg_ref[...]
        out_ref[rows, :] = (_sigmoid(o_ref[rows, :].astype(jnp.float32)) * h).astype(out_ref.dtype)
        return carry

    lax.fori_loop(0, nc, bwd, 0)


def mlstm(qk_c, vog, gates, bias, head_norm_g, batch, seq):
    T = batch * seq
    L = MLSTM_CHUNK
    nc = seq // L
    return pl.pallas_call(
        _mlstm_kernel, grid=(batch, M_HEADS),
        in_specs=[pl.BlockSpec((seq, M_QK_DIM), lambda b, h: (b, h)),
                  pl.BlockSpec((seq, M_QK_DIM), lambda b, h: (b, M_HEADS + h)),
                  pl.BlockSpec((seq, M_V_DIM), lambda b, h: (b, h)),
                  pl.BlockSpec((1, 1, nc, 4, L), lambda b, h: (b, h, 0, 0, 0)),
                  pl.BlockSpec((1, 4, 1), lambda b, h: (h, 0, 0)),
                  pl.BlockSpec((seq, M_V_DIM), lambda b, h: (b, M_HEADS + h)),
                  pl.BlockSpec((1, M_V_DIM), lambda b, h: (0, h))],
        out_specs=pl.BlockSpec((seq, M_V_DIM), lambda b, h: (b, h)),
        out_shape=jax.ShapeDtypeStruct((T, M_V), jnp.bfloat16),
        scratch_shapes=[pltpu.VMEM((seq, M_V_DIM), jnp.float32),
                        pltpu.VMEM((M_QK_DIM, M_V_DIM), jnp.float32),
                        pltpu.VMEM((1, M_QK_DIM), jnp.float32),
                        pltpu.VMEM((1, 1), jnp.float32)],
        compiler_params=_params("parallel", "parallel"), name="mlstm",
    )(qk_c, qk_c, vog, gates, bias, vog, head_norm_g.reshape(1, M_V))


def _rms(x, g_ref):
    return x * lax.rsqrt(jnp.mean(x * x, axis=-1, keepdims=True) + RMS_EPS) * g_ref[...]


def _rope_tile(t, cs):
    y = t * cs
    y = y + pltpu.roll(y, A_ROPE, 1)
    lane = lax.broadcasted_iota(jnp.int32, y.shape, 1)
    return jnp.where(lane < A_ROPE, y, 0.0)


def _mla_prep_kernel(cq_ref, ckv_ref, kr_ref, cs_ref, qg_ref, kvg_ref, cqn_ref, ckvn_ref, krope_ref):
    cqn_ref[...] = _rms(cq_ref[...], qg_ref).astype(jnp.bfloat16)
    ckvn_ref[...] = _rms(ckv_ref[...], kvg_ref).astype(jnp.bfloat16)
    krope_ref[...] = _rope_tile(kr_ref[...], cs_ref[...]).astype(jnp.bfloat16)


def mla_prep(small, cs, q_norm_g, kv_norm_g, seq):
    T = small.shape[0]
    tm = _tile(seq, 512)
    ns = seq // tm
    return pl.pallas_call(
        _mla_prep_kernel, grid=(T // tm,),
        in_specs=[pl.BlockSpec((tm, Q_LORA), lambda i: (i, 0)),
                  pl.BlockSpec((tm, KV_LORA), lambda i: (i, SMALL_CKV // KV_LORA)),
                  pl.BlockSpec((tm, V7X_LANES), lambda i: (i, SMALL_KR // V7X_LANES)),
                  pl.BlockSpec((tm, V7X_LANES), lambda i: (i % ns, 0)),
                  pl.BlockSpec((1, Q_LORA), lambda i: (0, 0)),
                  pl.BlockSpec((1, KV_LORA), lambda i: (0, 0))],
        out_specs=[pl.BlockSpec((tm, Q_LORA), lambda i: (i, 0)),
                   pl.BlockSpec((tm, KV_LORA), lambda i: (i, 0)),
                   pl.BlockSpec((tm, V7X_LANES), lambda i: (i, 0))],
        out_shape=[jax.ShapeDtypeStruct((T, Q_LORA), jnp.bfloat16),
                   jax.ShapeDtypeStruct((T, KV_LORA), jnp.bfloat16),
                   jax.ShapeDtypeStruct((T, V7X_LANES), jnp.bfloat16)],
        compiler_params=_params("parallel"), name="mla_prep",
    )(small, small, small, cs, q_norm_g.reshape(1, Q_LORA), kv_norm_g.reshape(1, KV_LORA))


Q_HEAD_W = 2 * V7X_LANES


def _q_up_kernel(a_ref, w_ref, cs_ref, o_ref):
    acc = jnp.dot(a_ref[...], w_ref[...], preferred_element_type=jnp.float32)
    cs = cs_ref[...]
    for hh in range(acc.shape[1] // Q_HEAD_W):
        c0 = hh * Q_HEAD_W
        o_ref[:, c0:c0 + A_NOPE] = (acc[:, c0:c0 + A_NOPE] * ATTN_SCALE).astype(o_ref.dtype)
        r = _rope_tile(acc[:, c0 + A_NOPE:c0 + Q_HEAD_W], cs)
        o_ref[:, c0 + A_NOPE:c0 + Q_HEAD_W] = (r * ATTN_SCALE).astype(o_ref.dtype)


def q_up(cqn, w_uq_r, cs, seq):
    T, K = cqn.shape
    N = w_uq_r.shape[1]
    tm = _tile(seq, 1024)
    tn = 1024
    ns = seq // tm
    return pl.pallas_call(
        _q_up_kernel, grid=(N // tn, T // tm),
        in_specs=[pl.BlockSpec((tm, K), lambda j, i: (i, 0)),
                  pl.BlockSpec((K, tn), lambda j, i: (0, j)),
                  pl.BlockSpec((tm, V7X_LANES), lambda j, i: (i % ns, 0))],
        out_specs=pl.BlockSpec((tm, tn), lambda j, i: (i, j)),
        out_shape=jax.ShapeDtypeStruct((T, N), jnp.bfloat16),
        compiler_params=_params("parallel", "parallel"), name="q_up",
    )(cqn, w_uq_r, cs)


def _attn_kernel(q_ref, kn_ref, kr_ref, v_ref, o_ref):
    q = q_ref[...]
    nt = (((1,), (1,)), ((), ()))
    s = lax.dot_general(q[:, :A_NOPE], kn_ref[...], nt, preferred_element_type=jnp.float32)
    s = s + lax.dot_general(q[:, A_NOPE:], kr_ref[...], nt, preferred_element_type=jnp.float32)
    m = jnp.max(s, axis=-1, keepdims=True)
    p = jnp.exp(s - m)
    l = jnp.sum(p, axis=-1, keepdims=True)
    o = jnp.dot(p.astype(jnp.bfloat16), v_ref[...], preferred_element_type=jnp.float32)
    o_ref[...] = (o / l).astype(o_ref.dtype)


def attention(q_cat, kv, krope, batch, seq):
    T = batch * seq
    tq = _tile(seq, 256)
    nq = seq // tq
    return pl.pallas_call(
        _attn_kernel, grid=(batch, A_HEADS, nq),
        in_specs=[pl.BlockSpec((tq, Q_HEAD_W), lambda b, h, i: (b * nq + i, h)),
                  pl.BlockSpec((seq, A_NOPE), lambda b, h, i: (b, 2 * h)),
                  pl.BlockSpec((seq, V7X_LANES), lambda b, h, i: (b, 0)),
                  pl.BlockSpec((seq, A_V), lambda b, h, i: (b, 2 * h + 1))],
        out_specs=pl.BlockSpec((tq, A_V), lambda b, h, i: (b * nq + i, h)),
        out_shape=jax.ShapeDtypeStruct((T, A_HEADS * A_V), jnp.bfloat16),
        compiler_params=_params("parallel", "parallel", "parallel"), name="attention",
    )(q_cat, kv, krope, kv)


def _merge_kernel(a_ref, b_ref, wa_ref, wb_ref, ga_ref, gb_ref, o_ref):
    ya = jnp.dot(a_ref[...], wa_ref[...], preferred_element_type=jnp.float32)
    yb = jnp.dot(b_ref[...], wb_ref[...], preferred_element_type=jnp.float32)
    y = _sigmoid(ga_ref[...].astype(jnp.float32)) * ya + _sigmoid(gb_ref[...].astype(jnp.float32)) * yb
    o_ref[...] = y.astype(o_ref.dtype)


def merge(hm, att, w_pm, w_pa, gates, tm=512, tn=512):
    T, K = hm.shape
    N = w_pm.shape[1]
    tm = _tile(T, tm)
    nb = N // tn
    a_spec = pl.BlockSpec((tm, K), lambda j, i: (i, 0))
    w_spec = pl.BlockSpec((K, tn), lambda j, i: (0, j))
    return pl.pallas_call(
        _merge_kernel, grid=(nb, T // tm),
        in_specs=[a_spec, a_spec, w_spec, w_spec,
                  pl.BlockSpec((tm, tn), lambda j, i: (i, 2 * nb + j)),
                  pl.BlockSpec((tm, tn), lambda j, i: (i, 3 * nb + j))],
        out_specs=pl.BlockSpec((tm, tn), lambda j, i: (i, j)),
        out_shape=jax.ShapeDtypeStruct((T, N), jnp.bfloat16),
        compiler_params=_params("parallel", "parallel"), name="merge",
    )(hm, att, w_pm, w_pa, gates, gates)


def _split3(x):
    hi = x.astype(jnp.bfloat16)
    r = x - hi.astype(jnp.float32)
    mid = r.astype(jnp.bfloat16)
    lo = (r - mid.astype(jnp.float32)).astype(jnp.bfloat16)
    return hi, mid, lo


def _ln_router_kernel(x_ref, y_ref, g_ref, b_ref, w_ref, rb_ref, xf_ref, ids_ref, wts_ref):
    x1 = _ln_body(DEEPNORM_ALPHA * x_ref[...] + y_ref[...], g_ref, b_ref)
    xf_ref[...] = x1
    xh, xm, xl = _split3(x1)
    wh, wm, wl = w_ref[0], w_ref[1], w_ref[2]
    dot = functools.partial(jnp.dot, preferred_element_type=jnp.float32)
    logits = (dot(xl, wh) + dot(xm, wm) + dot(xh, wl)) + (dot(xm, wh) + dot(xh, wm)) + dot(xh, wh)
    logits = logits + rb_ref[...]
    lane = lax.broadcasted_iota(jnp.int32, logits.shape, 1)
    neg = -jnp.inf
    gl = jnp.where(lane < N_GROUPS, logits, neg)
    ge = jnp.exp(gl - jnp.max(gl, axis=-1, keepdims=True))
    g_prob = ge / jnp.sum(ge, axis=-1, keepdims=True)
    g_top = jnp.max(g_prob, axis=-1, keepdims=True)
    g_idx = jnp.min(jnp.where(g_prob == g_top, lane, ROUTER_W), axis=-1, keepdims=True)
    e_lo = N_GROUPS + EXPERTS_PER_GROUP * g_idx
    e_in = jnp.where((lane >= e_lo) & (lane < e_lo + EXPERTS_PER_GROUP), logits, neg)
    v1 = jnp.max(e_in, axis=-1, keepdims=True)
    i1 = jnp.min(jnp.where(e_in == v1, lane, ROUTER_W), axis=-1, keepdims=True)
    e_in2 = jnp.where(lane == i1, neg, e_in)
    v2 = jnp.max(e_in2, axis=-1, keepdims=True)
    i2 = jnp.min(jnp.where(e_in2 == v2, lane, ROUTER_W), axis=-1, keepdims=True)
    e2 = jnp.exp(v2 - v1)
    w1 = 1.0 / (1.0 + e2) * g_top
    w2 = e2 / (1.0 + e2) * g_top
    ids_ref[...] = jnp.where(lane == 0, i1 - N_GROUPS, jnp.where(lane == 1, i2 - N_GROUPS, 0))
    wts_ref[...] = jnp.where(lane == 0, w1, jnp.where(lane == 1, w2, 0.0))


def ln_router(x, y, g, b, w_r3, b_r):
    T, D = x.shape
    tm = _tile(T, 256)
    row = pl.BlockSpec((tm, D), lambda i: (i, 0))
    vec = pl.BlockSpec((1, D), lambda i: (0, 0))
    small = pl.BlockSpec((tm, ROUTER_W), lambda i: (i, 0))
    return pl.pallas_call(
        _ln_router_kernel, grid=(T // tm,),
        in_specs=[row, row, vec, vec,
                  pl.BlockSpec((3, D, ROUTER_W), lambda i: (0, 0, 0)),
                  pl.BlockSpec((1, ROUTER_W), lambda i: (0, 0))],
        out_specs=[row, small, small],
        out_shape=[jax.ShapeDtypeStruct((T, D), jnp.float32),
                   jax.ShapeDtypeStruct((T, ROUTER_W), jnp.int32),
                   jax.ShapeDtypeStruct((T, ROUTER_W), jnp.float32)],
        compiler_params=_params("parallel"), name="ln_router",
    )(x, y, g.reshape(1, D), b.reshape(1, D), w_r3, b_r)


GATHER_ROWS = 128


def _gather_kernel(idx_ref, src_ref, o_ref, sem):
    base = pl.program_id(0) * GATHER_ROWS

    def row_copy(r, src_row):
        return pltpu.make_async_copy(src_ref.at[pl.ds(src_row, 1), :], o_ref.at[pl.ds(r, 1), :], sem)

    def start(r, carry):
        row_copy(r, idx_ref[base + r]).start()
        return carry

    def wait(r, carry):
        row_copy(r, 0).wait()
        return carry

    lax.fori_loop(0, GATHER_ROWS, start, 0)
    lax.fori_loop(0, GATHER_ROWS, wait, 0)


def gather_rows(src, idx):
    R = idx.shape[0]
    D = src.shape[1]
    return pl.pallas_call(
        _gather_kernel,
        grid_spec=pltpu.PrefetchScalarGridSpec(
            num_scalar_prefetch=1, grid=(R // GATHER_ROWS,),
            in_specs=[pl.BlockSpec(memory_space=pl.ANY)],
            out_specs=pl.BlockSpec((GATHER_ROWS, D), lambda i, idx: (i, 0)),
            scratch_shapes=[pltpu.SemaphoreType.DMA(())]),
        out_shape=jax.ShapeDtypeStruct((R, D), src.dtype),
        compiler_params=_params("arbitrary"), name="gather_rows",
    )(idx, src)


def _moe_up_kernel(te_ref, tv_ref, x_ref, wg_ref, wu_ref, h_ref):
    i = pl.program_id(1)

    @pl.when(tv_ref[i] == 1)
    def _():
        x = x_ref[...].astype(jnp.bfloat16)
        g = jnp.dot(x, wg_ref[0], preferred_element_type=jnp.float32)
        u = jnp.dot(x, wu_ref[0], preferred_element_type=jnp.float32)
        h_ref[...] = (g * _sigmoid(g) * u).astype(h_ref.dtype)

    @pl.when(tv_ref[i] == 0)
    def _():
        h_ref[...] = jnp.zeros_like(h_ref)


def moe_up(xs, w_gate, w_up, tile_e, tile_v, tn=512):
    P, D = xs.shape
    tm = MOE_TILE
    F = w_gate.shape[2]
    w_spec = pl.BlockSpec((1, D, tn), lambda j, i, te, tv: (te[i], 0, j))
    return pl.pallas_call(
        _moe_up_kernel,
        grid_spec=pltpu.PrefetchScalarGridSpec(
            num_scalar_prefetch=2, grid=(F // tn, P // tm),
            in_specs=[pl.BlockSpec((tm, D), lambda j, i, te, tv: (i, 0)), w_spec, w_spec],
            out_specs=pl.BlockSpec((tm, tn), lambda j, i, te, tv: (i, j))),
        out_shape=jax.ShapeDtypeStruct((P, F), jnp.bfloat16),
        compiler_params=_params("parallel", "arbitrary"), name="moe_up",
    )(tile_e, tile_v, xs, w_gate, w_up)


def _moe_down_kernel(te_ref, tv_ref, h_ref, wd_ref, c_ref, y_ref):
    i = pl.program_id(0)

    @pl.when(tv_ref[i] == 1)
    def _():
        y_ref[...] = c_ref[...] * jnp.dot(h_ref[...], wd_ref[0], preferred_element_type=jnp.float32)

    @pl.when(tv_ref[i] == 0)
    def _():
        y_ref[...] = jnp.zeros_like(y_ref)


def moe_down(h, w_down, row_w, tile_e, tile_v):
    P, F = h.shape
    tm = MOE_TILE
    D = w_down.shape[2]
    return pl.pallas_call(
        _moe_down_kernel,
        grid_spec=pltpu.PrefetchScalarGridSpec(
            num_scalar_prefetch=2, grid=(P // tm,),
            in_specs=[pl.BlockSpec((tm, F), lambda i, te, tv: (i, 0)),
                      pl.BlockSpec((1, F, D), lambda i, te, tv: (te[i], 0, 0)),
                      pl.BlockSpec((tm, 1), lambda i, te, tv: (i, 0))],
            out_specs=pl.BlockSpec((tm, D), lambda i, te, tv: (i, 0))),
        out_shape=jax.ShapeDtypeStruct((P, D), jnp.float32),
        compiler_params=_params("arbitrary"), name="moe_down",
    )(tile_e, tile_v, h, w_down, row_w)


def _ln_out_kernel(x_ref, y0_ref, y1_ref, g_ref, b_ref, o_ref):
    o_ref[...] = _ln_body(DEEPNORM_ALPHA * x_ref[...] + (y0_ref[...] + y1_ref[...]), g_ref, b_ref)


def ln_out(x, ysel, g, b):
    T, D = x.shape
    tm = _tile(T, 256)
    nb = T // tm
    row = pl.BlockSpec((tm, D), lambda i: (i, 0))
    vec = pl.BlockSpec((1, D), lambda i: (0, 0))
    return pl.pallas_call(
        _ln_out_kernel, grid=(nb,),
        in_specs=[row, row, pl.BlockSpec((tm, D), lambda i: (i + nb, 0)), vec, vec],
        out_specs=row, out_shape=jax.ShapeDtypeStruct((T, D), jnp.float32),
        compiler_params=_params("parallel"), name="ln_out",
    )(x, ysel, ysel, g.reshape(1, D), b.reshape(1, D))


def _rot_cols(w):
    half = w.shape[-1] // 2
    return jnp.concatenate([-w[..., half:], w[..., :half]], axis=-1)


def _prep_w_in(w_in):
    offs = np.cumsum((0,) + IN_SIZES)
    q, k, v, o, g, cq, ckv, kr, ga, gb = (w_in[:, offs[n]:offs[n + 1]] for n in range(len(IN_SIZES)))
    pad = jnp.zeros((w_in.shape[0], SMALL_W - SMALL_GATES - g.shape[1]), w_in.dtype)
    return jnp.concatenate([q, k, v, o, ga, gb, cq, ckv, kr, _rot_cols(kr), g, pad], axis=1).astype(jnp.bfloat16)


def _prep_w_uq(w_uq):
    w = w_uq.reshape(Q_LORA, A_HEADS, A_NOPE + A_ROPE)
    rope = w[:, :, A_NOPE:]
    w = jnp.concatenate([w[:, :, :A_NOPE], rope, _rot_cols(rope)], axis=-1)
    return w.reshape(Q_LORA, A_HEADS * Q_HEAD_W).astype(jnp.bfloat16)


def _rope_table(seq):
    inv = 1.0 / (ROPE_THETA ** (jnp.arange(0, A_ROPE, 2, dtype=jnp.float32) / A_ROPE))
    ang = jnp.arange(seq, dtype=jnp.float32)[:, None] * inv[None, :]
    cos, sin = jnp.cos(ang), jnp.sin(ang)
    return jnp.concatenate([cos, cos, sin, sin], axis=-1)


def _route_metadata(ids, wts, T):
    tm = MOE_TILE
    n_tiles = (TOP_K * T) // tm + N_EXPERTS
    P = n_tiles * tm
    eid = jnp.concatenate([ids[:, 0], ids[:, 1]])
    w = jnp.concatenate([wts[:, 0], wts[:, 1]])
    tok = jnp.concatenate([jnp.arange(T, dtype=jnp.int32)] * TOP_K)
    onehot = (eid[:, None] == jnp.arange(N_EXPERTS, dtype=jnp.int32)[None, :]).astype(jnp.int32)
    rank = jnp.sum((jnp.cumsum(onehot, axis=0) - onehot) * onehot, axis=1)
    counts = jnp.sum(onehot, axis=0)
    padded = ((counts + tm - 1) // tm) * tm
    ends = jnp.cumsum(padded)
    starts = ends - padded
    pos = (starts[eid] + rank).astype(jnp.int32)
    src = jnp.zeros((P,), jnp.int32).at[pos].set(tok)
    row_w = jnp.zeros((P,), jnp.float32).at[pos].set(w)
    tile_start = jnp.arange(n_tiles, dtype=jnp.int32) * tm
    tile_e = jnp.minimum(jnp.searchsorted(ends, tile_start, side="right"), N_EXPERTS - 1).astype(jnp.int32)
    tile_v = (tile_start < ends[-1]).astype(jnp.int32)
    return src, row_w.reshape(P, 1), pos, tile_e, tile_v


def kernel(x, ln0_g, ln0_b, w_in, b_gates, conv_qk, head_norm_g, q_norm_g, w_uq, kv_norm_g, w_ukv, w_pm, w_pa,
           w_out, ln1_g, ln1_b, w_rg, b_rg, w_re, b_re, w_e_gate, w_e_up, w_e_down, ln2_g, ln2_b):
    B, S, D = x.shape
    T = B * S
    bf16 = jnp.bfloat16
    cs = _rope_table(S)
    xf, xb = layer_norm_in(x.reshape(T, D), ln0_g, ln0_b)
    for l in range(DEPTH):
        w_in_r = _prep_w_in(w_in[l])
        qk = matmul(xb, w_in_r, jnp.float32, COL_QK, 2 * M_QK, name="proj_qk")
        v_o_g = matmul(xb, w_in_r, bf16, COL_V, M_V + M_V + 2 * D_MODEL, name="proj_vog")
        small = matmul(xb, w_in_r, jnp.float32, COL_SMALL, SMALL_W, name="proj_small")

        k_scale = jnp.concatenate([jnp.ones((1, M_QK), jnp.float32),
                                   jnp.full((1, M_QK), M_QK_DIM ** -0.5, jnp.float32)], axis=1)
        qk_c = conv_silu(qk, conv_qk[l], k_scale, S)
        nc = S // MLSTM_CHUNK
        gates = small[:, SMALL_GATES:SMALL_GATES + 4 * M_HEADS].reshape(B, nc, MLSTM_CHUNK, 4, M_HEADS)
        gates = gates.transpose(0, 4, 1, 3, 2)
        bias = b_gates[l].reshape(4, M_HEADS).T.reshape(M_HEADS, 4, 1)
        hm = mlstm(qk_c, v_o_g, gates, bias, head_norm_g[l], B, S)

        cqn, ckvn, krope = mla_prep(small, cs, q_norm_g[l], kv_norm_g[l], S)
        q_cat = q_up(cqn, _prep_w_uq(w_uq[l]), cs, S)
        kv = matmul(ckvn, w_ukv[l].astype(bf16), bf16, name="kv_up")
        att = attention(q_cat, kv, krope, B, S)

        merged = merge(hm, att, w_pm[l].astype(bf16), w_pa[l].astype(bf16), v_o_g)
        mixed = matmul(merged, w_out[l].astype(bf16), jnp.float32, name="proj_out")

        w_r = jnp.concatenate([w_rg[l], w_re[l],
                               jnp.zeros((D, ROUTER_W - N_GROUPS - N_EXPERTS), jnp.float32)], axis=1)
        w_r3 = jnp.stack(_split3(w_r))
        b_r = jnp.concatenate([b_rg[l], b_re[l], jnp.zeros((ROUTER_W - N_GROUPS - N_EXPERTS,), jnp.float32)])
        x1, ids, wts = ln_router(xf, mixed, ln1_g[l], ln1_b[l], w_r3, b_r.reshape(1, ROUTER_W))

        src, row_w, pos, tile_e, tile_v = _route_metadata(ids, wts, T)
        xs = gather_rows(x1, src)
        h = moe_up(xs, w_e_gate[l].astype(bf16), w_e_up[l].astype(bf16), tile_e, tile_v)
        y = moe_down(h, w_e_down[l].astype(bf16), row_w, tile_e, tile_v)
        ysel = gather_rows(y, pos)
        xf = ln_out(x1, ysel, ln2_g[l], ln2_b[l])
        if l + 1 < DEPTH:
            xb = xf.astype(bf16)
    return xf.reshape(B, S, D)
```

```python
import functools
import math

import jax
import jax.numpy as jnp
import numpy as np
from jax import lax
from jax.experimental import pallas as pl
from jax.experimental.pallas import tpu as pltpu

D_MODEL = 4096
M_HEADS = 8
M_QK_DIM = 256
M_V_DIM = 512
M_QK = M_HEADS * M_QK_DIM
M_V = M_HEADS * M_V_DIM
A_HEADS = 32
A_NOPE = 128
A_ROPE = 64
A_V = 128
Q_LORA = 1024
KV_LORA = 512
ROPE_THETA = 10000.0
ATTN_SCALE = (A_NOPE + A_ROPE) ** -0.5
N_GROUPS = 8
EXPERTS_PER_GROUP = 4
N_EXPERTS = 32
TOP_K = 2
D_EXPERT = 1024
LN_EPS = 1e-5
RMS_EPS = 1e-6
DEPTH = 1
DEEPNORM_ALPHA = (2 * DEPTH) ** 0.25
IN_SIZES = (M_QK, M_QK, M_V, M_V, 4 * M_HEADS, Q_LORA, KV_LORA, A_ROPE, D_MODEL, D_MODEL)

V7X_LANES = 128
V7X_VMEM_LIMIT = 56 * 1024 * 1024

MLSTM_CHUNK = 256
MOE_TILE = 512
SMALL_W = 2048
SMALL_CKV = Q_LORA
SMALL_KR = Q_LORA + KV_LORA
SMALL_GATES = SMALL_KR + 2 * A_ROPE
COL_QK = 0
COL_V = 2 * M_QK
COL_O = COL_V + M_V
COL_GA = COL_O + M_V
COL_GB = COL_GA + D_MODEL
COL_SMALL = COL_GB + D_MODEL
ROUTER_W = 128


def _params(*sem):
    return pltpu.CompilerParams(dimension_semantics=sem, vmem_limit_bytes=V7X_VMEM_LIMIT)


def _tile(n, pref):
    t = min(n, pref)
    while n % t:
        t //= 2
    return t


def _sigmoid(x):
    return 1.0 / (1.0 + jnp.exp(-x))


def _ln_body(z, g_ref, b_ref):
    mu = jnp.mean(z, axis=-1, keepdims=True)
    zc = z - mu
    var = jnp.mean(zc * zc, axis=-1, keepdims=True)
    return zc * lax.rsqrt(var + LN_EPS) * g_ref[...] + b_ref[...]


def _ln_kernel(x_ref, g_ref, b_ref, of_ref, ob_ref):
    y = _ln_body(x_ref[...], g_ref, b_ref)
    of_ref[...] = y
    ob_ref[...] = y.astype(jnp.bfloat16)


def layer_norm_in(x, g, b):
    T, D = x.shape
    tm = _tile(T, 256)
    row = pl.BlockSpec((tm, D), lambda i: (i, 0))
    vec = pl.BlockSpec((1, D), lambda i: (0, 0))
    return pl.pallas_call(
        _ln_kernel, grid=(T // tm,), in_specs=[row, vec, vec], out_specs=[row, row],
        out_shape=[jax.ShapeDtypeStruct((T, D), jnp.float32), jax.ShapeDtypeStruct((T, D), jnp.bfloat16)],
        compiler_params=_params("parallel"), name="ln_in",
    )(x, g.reshape(1, D), b.reshape(1, D))


def _mm_kernel(a_ref, w_ref, o_ref):
    o_ref[...] = jnp.dot(a_ref[...], w_ref[...], preferred_element_type=jnp.float32).astype(o_ref.dtype)


def matmul(a, w, out_dtype, col0=0, ncols=None, tm=1024, tn=1024, name="mm"):
    M, K = a.shape
    ncols = w.shape[1] - col0 if ncols is None else ncols
    tm = _tile(M, tm)
    tn = _tile(ncols, tn)
    assert col0 % tn == 0
    cb = col0 // tn
    return pl.pallas_call(
        _mm_kernel, grid=(ncols // tn, M // tm),
        in_specs=[pl.BlockSpec((tm, K), lambda j, i: (i, 0)),
                  pl.BlockSpec((K, tn), lambda j, i: (0, j + cb))],
        out_specs=pl.BlockSpec((tm, tn), lambda j, i: (i, j)),
        out_shape=jax.ShapeDtypeStruct((M, ncols), out_dtype),
        compiler_params=_params("parallel", "parallel"), name=name,
    )(a, w)


_NT = (((1,), (1,)), ((), ()))


def _mm_nt_kernel(a_ref, w_ref, o_ref):
    o_ref[...] = lax.dot_general(a_ref[...], w_ref[...], _NT,
                                 preferred_element_type=jnp.float32).astype(o_ref.dtype)


def matmul_nt(a, wt, out_dtype, row0, nrows, tm=1024, tn=1024, name="mm_nt"):
    M, K = a.shape
    tm = _tile(M, tm)
    tn = _tile(nrows, tn)
    assert row0 % tn == 0
    rb = row0 // tn
    return pl.pallas_call(
        _mm_nt_kernel, grid=(nrows // tn, M // tm),
        in_specs=[pl.BlockSpec((tm, K), lambda j, i: (i, 0)),
                  pl.BlockSpec((tn, K), lambda j, i: (j + rb, 0))],
        out_specs=pl.BlockSpec((tm, tn), lambda j, i: (i, j)),
        out_shape=jax.ShapeDtypeStruct((M, nrows), out_dtype),
        compiler_params=_params("parallel", "parallel"), name=name,
    )(a, wt)


def _conv_silu_kernel(seq, x_ref, prev_ref, next_ref, w_ref, s_ref, o_ref):
    tm = x_ref.shape[0]
    i = pl.program_id(0)
    x = x_ref[...]
    row = lax.broadcasted_iota(jnp.int32, x.shape, 0)
    first = (i * tm) % seq == 0
    last = ((i + 1) * tm) % seq == 0
    prev_row = jnp.where(first, 0.0, prev_ref[7:8, :])
    next_row = jnp.where(last, 0.0, next_ref[0:1, :])
    x_prev = jnp.where(row == 0, prev_row, pltpu.roll(x, 1, 0))
    x_next = jnp.where(row == tm - 1, next_row, pltpu.roll(x, tm - 1, 0))
    y = w_ref[0:1, :] * x_prev + w_ref[1:2, :] * x + w_ref[2:3, :] * x_next
    y = y * _sigmoid(y) * s_ref[...]
    o_ref[...] = y.astype(o_ref.dtype)


def conv_silu(qk, conv_w, col_scale, seq):
    T, C = qk.shape
    tm = _tile(seq, 512)
    tc = _tile(C, 1024)
    nb8 = T // 8
    return pl.pallas_call(
        functools.partial(_conv_silu_kernel, seq), grid=(T // tm, C // tc),
        in_specs=[pl.BlockSpec((tm, tc), lambda i, j: (i, j)),
                  pl.BlockSpec((8, tc), lambda i, j: (jnp.maximum(i * (tm // 8) - 1, 0), j)),
                  pl.BlockSpec((8, tc), lambda i, j: (jnp.minimum((i + 1) * (tm // 8), nb8 - 1), j)),
                  pl.BlockSpec((3, tc), lambda i, j: (0, j)),
                  pl.BlockSpec((1, tc), lambda i, j: (0, j))],
        out_specs=pl.BlockSpec((tm, tc), lambda i, j: (i, j)),
        out_shape=jax.ShapeDtypeStruct((T, C), jnp.bfloat16),
        compiler_params=_params("parallel", "parallel"), name="conv_silu",
    )(qk, qk, qk, conv_w, col_scale)


def _log_sigmoid(x):
    return jnp.minimum(x, 0.0) - jnp.log(1.0 + jnp.exp(-jnp.abs(x)))


def _mlstm_chunk(q, k, v, gi_row, gf_row, c_ref, n_ref, m_ref, reverse):
    L = q.shape[0]
    t_idx = lax.broadcasted_iota(jnp.int32, (L, L), 0)
    s_idx = lax.broadcasted_iota(jnp.int32, (L, L), 1)
    seen = (s_idx >= t_idx) if reverse else (s_idx <= t_idx)
    eye = s_idx == t_idx
    f_row = _log_sigmoid(gf_row)
    b_col = jnp.sum(jnp.where(seen, f_row, 0.0), axis=1, keepdims=True)
    b_row = jnp.sum(jnp.where(eye, b_col, 0.0), axis=0, keepdims=True)
    i_col = jnp.sum(jnp.where(eye, gi_row, 0.0), axis=1, keepdims=True)
    b_last = jnp.sum(f_row, axis=1, keepdims=True)
    m_prev = m_ref[...]

    dmat = jnp.where(seen, b_col - b_row + gi_row, -jnp.inf)
    inter = b_col + m_prev
    m_t = jnp.maximum(inter, jnp.max(dmat, axis=1, keepdims=True))
    w_inter = jnp.exp(inter - m_t)
    s_qk = lax.dot_general(q, k, (((1,), (1,)), ((), ())), preferred_element_type=jnp.float32)
    s_qk = s_qk * jnp.exp(dmat - m_t)
    c_prev = c_ref[...]
    num = w_inter * jnp.dot(q, c_prev.astype(jnp.bfloat16), preferred_element_type=jnp.float32)
    num = num + jnp.dot(s_qk.astype(jnp.bfloat16), v, preferred_element_type=jnp.float32)
    qn = jnp.sum(q.astype(jnp.float32) * n_ref[...], axis=1, keepdims=True)
    den = w_inter * qn + jnp.sum(s_qk, axis=1, keepdims=True)
    h = num / jnp.maximum(jnp.abs(den), jnp.exp(-m_t))

    dec_col = b_last - b_col + i_col
    m_new = jnp.maximum(b_last + m_prev, jnp.max(dec_col, axis=0, keepdims=True))
    a = jnp.exp(b_last + m_prev - m_new)
    kw = k.astype(jnp.float32) * jnp.exp(dec_col - m_new)
    c_ref[...] = a * c_prev + lax.dot_general(
        kw.astype(jnp.bfloat16), v, (((0,), (0,)), ((), ())), preferred_element_type=jnp.float32)
    n_ref[...] = a * n_ref[...] + jnp.sum(kw, axis=0, keepdims=True)
    m_ref[...] = m_new
    return h


def _mlstm_kernel(q_ref, k_ref, v_ref, g_ref, bias_ref, o_ref, hg_ref, out_ref, hf_ref, c_ref, n_ref, m_ref):
    L = MLSTM_CHUNK
    nc = q_ref.shape[0] // L

    def reset():
        c_ref[...] = jnp.zeros_like(c_ref)
        n_ref[...] = jnp.zeros_like(n_ref)
        m_ref[...] = jnp.zeros_like(m_ref)

    def chunk(c, reverse):
        r0 = pl.multiple_of(c * L, L)
        rows = pl.ds(r0, L)
        g = g_ref[0, 0, c] + bias_ref[0]
        gi, gf = (g[2:3], g[3:4]) if reverse else (g[0:1], g[1:2])
        h = _mlstm_chunk(q_ref[rows, :], k_ref[rows, :], v_ref[rows, :], gi, gf, c_ref, n_ref, m_ref, reverse)
        return rows, h

    reset()

    def fwd(c, carry):
        rows, h = chunk(c, False)
        hf_ref[rows, :] = h
        return carry

    lax.fori_loop(0, nc, fwd, 0)
    reset()

    def bwd(j, carry):
        rows, h = chunk(nc - 1 - j, True)
        h = h + hf_ref[rows, :]
        h = h * lax.rsqrt(jnp.mean(h * h, axis=-1, keepdims=True) + RMS_EPS) * hg_ref[...]
        out_ref[rows, :] = (_sigmoid(o_ref[rows, :].astype(jnp.float32)) * h).astype(out_ref.dtype)
        return carry

    lax.fori_loop(0, nc, bwd, 0)


def mlstm(qk_c, vog, gates, bias, head_norm_g, batch, seq):
    T = batch * seq
    L = MLSTM_CHUNK
    nc = seq // L
    return pl.pallas_call(
        _mlstm_kernel, grid=(batch, M_HEADS),
        in_specs=[pl.BlockSpec((seq, M_QK_DIM), lambda b, h: (b, h)),
                  pl.BlockSpec((seq, M_QK_DIM), lambda b, h: (b, M_HEADS + h)),
                  pl.BlockSpec((seq, M_V_DIM), lambda b, h: (b, h)),
                  pl.BlockSpec((1, 1, nc, 4, L), lambda b, h: (b, h, 0, 0, 0)),
                  pl.BlockSpec((1, 4, 1), lambda b, h: (h, 0, 0)),
                  pl.BlockSpec((seq, M_V_DIM), lambda b, h: (b, M_HEADS + h)),
                  pl.BlockSpec((1, M_V_DIM), lambda b, h: (0, h))],
        out_specs=pl.BlockSpec((seq, M_V_DIM), lambda b, h: (b, h)),
        out_shape=jax.ShapeDtypeStruct((T, M_V), jnp.bfloat16),
        scratch_shapes=[pltpu.VMEM((seq, M_V_DIM), jnp.float32),
                        pltpu.VMEM((M_QK_DIM, M_V_DIM), jnp.float32),
                        pltpu.VMEM((1, M_QK_DIM), jnp.float32),
                        pltpu.VMEM((1, 1), jnp.float32)],
        compiler_params=_params("parallel", "parallel"), name="mlstm",
    )(qk_c, qk_c, vog, gates, bias, vog, head_norm_g.reshape(1, M_V))


def _rms(x, g_ref):
    return x * lax.rsqrt(jnp.mean(x * x, axis=-1, keepdims=True) + RMS_EPS) * g_ref[...]


def _rope_tile(t, cs):
    y = t * cs
    y = y + pltpu.roll(y, A_ROPE, 1)
    lane = lax.broadcasted_iota(jnp.int32, y.shape, 1)
    return jnp.where(lane < A_ROPE, y, 0.0)


def _mla_prep_kernel(cq_ref, ckv_ref, kr_ref, cs_ref, qg_ref, kvg_ref, cqn_ref, ckvn_ref, krope_ref):
    cqn_ref[...] = _rms(cq_ref[...], qg_ref).astype(jnp.bfloat16)
    ckvn_ref[...] = _rms(ckv_ref[...], kvg_ref).astype(jnp.bfloat16)
    krope_ref[...] = _rope_tile(kr_ref[...], cs_ref[...]).astype(jnp.bfloat16)


def mla_prep(small, cs, q_norm_g, kv_norm_g, seq):
    T = small.shape[0]
    tm = _tile(seq, 512)
    ns = seq // tm
    return pl.pallas_call(
        _mla_prep_kernel, grid=(T // tm,),
        in_specs=[pl.BlockSpec((tm, Q_LORA), lambda i: (i, 0)),
                  pl.BlockSpec((tm, KV_LORA), lambda i: (i, SMALL_CKV // KV_LORA)),
                  pl.BlockSpec((tm, V7X_LANES), lambda i: (i, SMALL_KR // V7X_LANES)),
                  pl.BlockSpec((tm, V7X_LANES), lambda i: (i % ns, 0)),
                  pl.BlockSpec((1, Q_LORA), lambda i: (0, 0)),
                  pl.BlockSpec((1, KV_LORA), lambda i: (0, 0))],
        out_specs=[pl.BlockSpec((tm, Q_LORA), lambda i: (i, 0)),
                   pl.BlockSpec((tm, KV_LORA), lambda i: (i, 0)),
                   pl.BlockSpec((tm, V7X_LANES), lambda i: (i, 0))],
        out_shape=[jax.ShapeDtypeStruct((T, Q_LORA), jnp.bfloat16),
                   jax.ShapeDtypeStruct((T, KV_LORA), jnp.bfloat16),
                   jax.ShapeDtypeStruct((T, V7X_LANES), jnp.bfloat16)],
        compiler_params=_params("parallel"), name="mla_prep",
    )(small, small, small, cs, q_norm_g.reshape(1, Q_LORA), kv_norm_g.reshape(1, KV_LORA))


Q_HEAD_W = 2 * V7X_LANES
Q_SCALE = ATTN_SCALE * math.log2(math.e)


def _q_up_kernel(a_ref, w_ref, cs_ref, o_ref):
    acc = jnp.dot(a_ref[...], w_ref[...], preferred_element_type=jnp.float32)
    cs = cs_ref[...]
    for hh in range(acc.shape[1] // Q_HEAD_W):
        c0 = hh * Q_HEAD_W
        o_ref[:, c0:c0 + A_NOPE] = (acc[:, c0:c0 + A_NOPE] * Q_SCALE).astype(o_ref.dtype)
        r = _rope_tile(acc[:, c0 + A_NOPE:c0 + Q_HEAD_W], cs)
        o_ref[:, c0 + A_NOPE:c0 + Q_HEAD_W] = (r * Q_SCALE).astype(o_ref.dtype)


def q_up(cqn, w_uq_r, cs, seq):
    T, K = cqn.shape
    N = w_uq_r.shape[1]
    tm = _tile(seq, 1024)
    tn = 1024
    ns = seq // tm
    return pl.pallas_call(
        _q_up_kernel, grid=(N // tn, T // tm),
        in_specs=[pl.BlockSpec((tm, K), lambda j, i: (i, 0)),
                  pl.BlockSpec((K, tn), lambda j, i: (0, j)),
                  pl.BlockSpec((tm, V7X_LANES), lambda j, i: (i % ns, 0))],
        out_specs=pl.BlockSpec((tm, tn), lambda j, i: (i, j)),
        out_shape=jax.ShapeDtypeStruct((T, N), jnp.bfloat16),
        compiler_params=_params("parallel", "parallel"), name="q_up",
    )(cqn, w_uq_r, cs)


ATTN_Q_TILE = 512
ATTN_KV_CHUNK = 512


def _attn_kernel(q_ref, kn_ref, kr_ref, v_ref, o_ref, kcat_ref, vcat_ref):
    @pl.when(pl.program_id(2) == 0)
    def _():
        kcat_ref[:, :A_NOPE] = kn_ref[...]
        kcat_ref[:, A_NOPE:] = kr_ref[...]
        vcat_ref[:, :A_V] = v_ref[...]
        vcat_ref[:, A_V:] = jnp.ones((vcat_ref.shape[0], vcat_ref.shape[1] - A_V), vcat_ref.dtype)

    q = q_ref[...]
    seq = kcat_ref.shape[0]
    ck = min(ATTN_KV_CHUNK, seq)
    m = jnp.full((q.shape[0], 1), -jnp.inf, jnp.float32)
    acc = jnp.zeros((q.shape[0], vcat_ref.shape[1]), jnp.float32)
    for c in range(seq // ck):
        rows = slice(c * ck, (c + 1) * ck)
        s = lax.dot_general(q, kcat_ref[rows, :], _NT, preferred_element_type=jnp.float32)
        m_new = jnp.maximum(m, jnp.max(s, axis=-1, keepdims=True))
        p = jnp.exp2(s - m_new).astype(jnp.bfloat16)
        acc = jnp.exp2(m - m_new) * acc + jnp.dot(p, vcat_ref[rows, :], preferred_element_type=jnp.float32)
        m = m_new
    o_ref[...] = (acc[:, :A_V] / acc[:, A_V:]).astype(o_ref.dtype)


def attention(q_cat, kv, krope, batch, seq):
    T = batch * seq
    tq = _tile(seq, ATTN_Q_TILE)
    nq = seq // tq
    return pl.pallas_call(
        _attn_kernel, grid=(batch, A_HEADS, nq),
        in_specs=[pl.BlockSpec((tq, Q_HEAD_W), lambda b, h, i: (b * nq + i, h)),
                  pl.BlockSpec((seq, A_NOPE), lambda b, h, i: (b, 2 * h)),
                  pl.BlockSpec((seq, V7X_LANES), lambda b, h, i: (b, 0)),
                  pl.BlockSpec((seq, A_V), lambda b, h, i: (b, 2 * h + 1))],
        out_specs=pl.BlockSpec((tq, A_V), lambda b, h, i: (b * nq + i, h)),
        out_shape=jax.ShapeDtypeStruct((T, A_HEADS * A_V), jnp.bfloat16),
        scratch_shapes=[pltpu.VMEM((seq, 2 * V7X_LANES), jnp.bfloat16),
                        pltpu.VMEM((seq, 2 * A_V), jnp.bfloat16)],
        compiler_params=_params("parallel", "parallel", "arbitrary"), name="attention",
    )(q_cat, kv, krope, kv)


def _merge_kernel(a_ref, b_ref, wa_ref, wb_ref, ga_ref, gb_ref, o_ref):
    ya = jnp.dot(a_ref[...], wa_ref[...], preferred_element_type=jnp.float32)
    yb = jnp.dot(b_ref[...], wb_ref[...], preferred_element_type=jnp.float32)
    y = _sigmoid(ga_ref[...].astype(jnp.float32)) * ya + _sigmoid(gb_ref[...].astype(jnp.float32)) * yb
    o_ref[...] = y.astype(o_ref.dtype)


def merge(hm, att, w_pm, w_pa, gates, tm=512, tn=512):
    T, K = hm.shape
    N = w_pm.shape[1]
    tm = _tile(T, tm)
    nb = N // tn
    a_spec = pl.BlockSpec((tm, K), lambda j, i: (i, 0))
    w_spec = pl.BlockSpec((K, tn), lambda j, i: (0, j))
    return pl.pallas_call(
        _merge_kernel, grid=(nb, T // tm),
        in_specs=[a_spec, a_spec, w_spec, w_spec,
                  pl.BlockSpec((tm, tn), lambda j, i: (i, 2 * nb + j)),
                  pl.BlockSpec((tm, tn), lambda j, i: (i, 3 * nb + j))],
        out_specs=pl.BlockSpec((tm, tn), lambda j, i: (i, j)),
        out_shape=jax.ShapeDtypeStruct((T, N), jnp.bfloat16),
        compiler_params=_params("parallel", "parallel"), name="merge",
    )(hm, att, w_pm, w_pa, gates, gates)


def _split3(x):
    hi = x.astype(jnp.bfloat16)
    r = x - hi.astype(jnp.float32)
    mid = r.astype(jnp.bfloat16)
    lo = (r - mid.astype(jnp.float32)).astype(jnp.bfloat16)
    return hi, mid, lo


_HI16 = 0xFFFF0000


def _pack_bf16_halves(xb):
    bits = pltpu.bitcast(xb.astype(jnp.float32), jnp.uint32)
    half = xb.shape[1] // 2
    return (bits[:, :half] >> 16) | (bits[:, half:] & jnp.uint32(_HI16))


def _unpack_bf16_halves(u):
    lo = pltpu.bitcast(u << 16, jnp.float32).astype(jnp.bfloat16)
    hi = pltpu.bitcast(u & jnp.uint32(_HI16), jnp.float32).astype(jnp.bfloat16)
    return lo, hi


def _ln_router_kernel(x_ref, y_ref, g_ref, b_ref, w_ref, rb_ref, xf_ref, xp_ref, ids_ref, wts_ref):
    x1 = _ln_body(DEEPNORM_ALPHA * x_ref[...] + y_ref[...], g_ref, b_ref)
    xf_ref[...] = x1
    xh, xm, xl = _split3(x1)
    xp_ref[...] = _pack_bf16_halves(xh)
    wh, wm, wl = w_ref[0], w_ref[1], w_ref[2]
    dot = functools.partial(jnp.dot, preferred_element_type=jnp.float32)
    logits = (dot(xl, wh) + dot(xm, wm) + dot(xh, wl)) + (dot(xm, wh) + dot(xh, wm)) + dot(xh, wh)
    logits = logits + rb_ref[...]
    lane = lax.broadcasted_iota(jnp.int32, logits.shape, 1)
    neg = -jnp.inf
    gl = jnp.where(lane < N_GROUPS, logits, neg)
    ge = jnp.exp(gl - jnp.max(gl, axis=-1, keepdims=True))
    g_prob = ge / jnp.sum(ge, axis=-1, keepdims=True)
    g_top = jnp.max(g_prob, axis=-1, keepdims=True)
    g_idx = jnp.min(jnp.where(g_prob == g_top, lane, ROUTER_W), axis=-1, keepdims=True)
    e_lo = N_GROUPS + EXPERTS_PER_GROUP * g_idx
    e_in = jnp.where((lane >= e_lo) & (lane < e_lo + EXPERTS_PER_GROUP), logits, neg)
    v1 = jnp.max(e_in, axis=-1, keepdims=True)
    i1 = jnp.min(jnp.where(e_in == v1, lane, ROUTER_W), axis=-1, keepdims=True)
    e_in2 = jnp.where(lane == i1, neg, e_in)
    v2 = jnp.max(e_in2, axis=-1, keepdims=True)
    i2 = jnp.min(jnp.where(e_in2 == v2, lane, ROUTER_W), axis=-1, keepdims=True)
    e2 = jnp.exp(v2 - v1)
    w1 = 1.0 / (1.0 + e2) * g_top
    w2 = e2 / (1.0 + e2) * g_top
    ids_ref[...] = jnp.where(lane == 0, i1 - N_GROUPS, jnp.where(lane == 1, i2 - N_GROUPS, 0))
    wts_ref[...] = jnp.where(lane == 0, w1, jnp.where(lane == 1, w2, 0.0))


def ln_router(x, y, g, b, w_r3, b_r):
    T, D = x.shape
    tm = _tile(T, 256)
    row = pl.BlockSpec((tm, D), lambda i: (i, 0))
    vec = pl.BlockSpec((1, D), lambda i: (0, 0))
    small = pl.BlockSpec((tm, ROUTER_W), lambda i: (i, 0))
    return pl.pallas_call(
        _ln_router_kernel, grid=(T // tm,),
        in_specs=[row, row, vec, vec,
                  pl.BlockSpec((3, D, ROUTER_W), lambda i: (0, 0, 0)),
                  pl.BlockSpec((1, ROUTER_W), lambda i: (0, 0))],
        out_specs=[row, pl.BlockSpec((tm, D // 2), lambda i: (i, 0)), small, small],
        out_shape=[jax.ShapeDtypeStruct((T, D), jnp.float32),
                   jax.ShapeDtypeStruct((T, D // 2), jnp.uint32),
                   jax.ShapeDtypeStruct((T, ROUTER_W), jnp.int32),
                   jax.ShapeDtypeStruct((T, ROUTER_W), jnp.float32)],
        compiler_params=_params("parallel"), name="ln_router",
    )(x, y, g.reshape(1, D), b.reshape(1, D), w_r3, b_r)


GATHER_UNROLL = 8


def _gather_kernel(rows, idx_ref, src_ref, o_ref, sem):
    base = pl.program_id(0) * rows

    def row_copy(r, src_row):
        return pltpu.make_async_copy(src_ref.at[pl.ds(src_row, 1), :], o_ref.at[pl.ds(r, 1), :], sem)

    def start(r, carry):
        row_copy(r, idx_ref[base + r]).start()
        return carry

    def wait(r, carry):
        row_copy(r, 0).wait()
        return carry

    lax.fori_loop(0, rows, start, 0, unroll=GATHER_UNROLL)
    lax.fori_loop(0, rows, wait, 0, unroll=GATHER_UNROLL)


def gather_rows(src, idx, rows):
    R = idx.shape[0]
    D = src.shape[1]
    assert R % rows == 0 and rows % GATHER_UNROLL == 0
    return pl.pallas_call(
        functools.partial(_gather_kernel, rows),
        grid_spec=pltpu.PrefetchScalarGridSpec(
            num_scalar_prefetch=1, grid=(R // rows,),
            in_specs=[pl.BlockSpec(memory_space=pl.ANY)],
            out_specs=pl.BlockSpec((rows, D), lambda i, idx: (i, 0)),
            scratch_shapes=[pltpu.SemaphoreType.DMA(())]),
        out_shape=jax.ShapeDtypeStruct((R, D), src.dtype),
        compiler_params=_params("arbitrary"), name="gather_rows",
    )(idx, src)


def _expert_changed(te_ref, i):
    return (i == 0) | (te_ref[i] != te_ref[jnp.maximum(i - 1, 0)])


def _moe_up_kernel(te_ref, tv_ref, x_ref, wg_ref, wu_ref, h_ref, wgb_ref, wub_ref):
    i = pl.program_id(1)

    @pl.when(_expert_changed(te_ref, i))
    def _():
        wgb_ref[...] = wg_ref[0].astype(jnp.bfloat16)
        wub_ref[...] = wu_ref[0].astype(jnp.bfloat16)

    @pl.when(tv_ref[i] == 1)
    def _():
        x_lo, x_hi = _unpack_bf16_halves(x_ref[...])
        half = x_lo.shape[1]
        dot = functools.partial(jnp.dot, preferred_element_type=jnp.float32)
        g = dot(x_lo, wgb_ref[:half, :]) + dot(x_hi, wgb_ref[half:, :])
        u = dot(x_lo, wub_ref[:half, :]) + dot(x_hi, wub_ref[half:, :])
        h_ref[...] = (g * _sigmoid(g) * u).astype(h_ref.dtype)

    @pl.when(tv_ref[i] == 0)
    def _():
        h_ref[...] = jnp.zeros_like(h_ref)


def moe_up(xs, w_gate, w_up, tile_e, tile_v, tn=256):
    P = xs.shape[0]
    tm = MOE_TILE
    _, D, F = w_gate.shape
    w_spec = pl.BlockSpec((1, D, tn), lambda j, i, te, tv: (te[i], 0, j))
    return pl.pallas_call(
        _moe_up_kernel,
        grid_spec=pltpu.PrefetchScalarGridSpec(
            num_scalar_prefetch=2, grid=(F // tn, P // tm),
            in_specs=[pl.BlockSpec((tm, D // 2), lambda j, i, te, tv: (i, 0)), w_spec, w_spec],
            out_specs=pl.BlockSpec((tm, tn), lambda j, i, te, tv: (i, j)),
            scratch_shapes=[pltpu.VMEM((D, tn), jnp.bfloat16), pltpu.VMEM((D, tn), jnp.bfloat16)]),
        out_shape=jax.ShapeDtypeStruct((P, F), jnp.bfloat16),
        compiler_params=_params("arbitrary", "arbitrary"), name="moe_up",
    )(tile_e, tile_v, xs, w_gate, w_up)


def _moe_down_kernel(te_ref, tv_ref, h_ref, wd_ref, c_ref, y_ref, wdb_ref):
    i = pl.program_id(1)

    @pl.when(_expert_changed(te_ref, i))
    def _():
        wdb_ref[...] = wd_ref[0].astype(jnp.bfloat16)

    @pl.when(tv_ref[i] == 1)
    def _():
        y_ref[...] = c_ref[...] * jnp.dot(h_ref[...], wdb_ref[...], preferred_element_type=jnp.float32)

    @pl.when(tv_ref[i] == 0)
    def _():
        y_ref[...] = jnp.zeros_like(y_ref)


def moe_down(h, w_down, row_w, tile_e, tile_v, tn=2048):
    P, F = h.shape
    tm = MOE_TILE
    D = w_down.shape[2]
    return pl.pallas_call(
        _moe_down_kernel,
        grid_spec=pltpu.PrefetchScalarGridSpec(
            num_scalar_prefetch=2, grid=(D // tn, P // tm),
            in_specs=[pl.BlockSpec((tm, F), lambda j, i, te, tv: (i, 0)),
                      pl.BlockSpec((1, F, tn), lambda j, i, te, tv: (te[i], 0, j)),
                      pl.BlockSpec((tm, 1), lambda j, i, te, tv: (i, 0))],
            out_specs=pl.BlockSpec((tm, tn), lambda j, i, te, tv: (i, j)),
            scratch_shapes=[pltpu.VMEM((F, tn), jnp.bfloat16)]),
        out_shape=jax.ShapeDtypeStruct((P, D), jnp.float32),
        compiler_params=_params("arbitrary", "arbitrary"), name="moe_down",
    )(tile_e, tile_v, h, w_down, row_w)


def _ln_out_kernel(x_ref, y0_ref, y1_ref, g_ref, b_ref, o_ref):
    o_ref[...] = _ln_body(DEEPNORM_ALPHA * x_ref[...] + (y0_ref[...] + y1_ref[...]), g_ref, b_ref)


def ln_out(x, ysel, g, b):
    T, D = x.shape
    tm = _tile(T, 256)
    nb = T // tm
    row = pl.BlockSpec((tm, D), lambda i: (i, 0))
    vec = pl.BlockSpec((1, D), lambda i: (0, 0))
    return pl.pallas_call(
        _ln_out_kernel, grid=(nb,),
        in_specs=[row, row, pl.BlockSpec((tm, D), lambda i: (i + nb, 0)), vec, vec],
        out_specs=row, out_shape=jax.ShapeDtypeStruct((T, D), jnp.float32),
        compiler_params=_params("parallel"), name="ln_out",
    )(x, ysel, ysel, g.reshape(1, D), b.reshape(1, D))


def _rot_cols(w):
    half = w.shape[-1] // 2
    return jnp.concatenate([-w[..., half:], w[..., :half]], axis=-1)


def _prep_w_in_t(w_in):
    wt = w_in.T
    offs = np.cumsum((0,) + IN_SIZES)
    q, k, v, o, g, cq, ckv, kr, ga, gb = (wt[offs[n]:offs[n + 1]] for n in range(len(IN_SIZES)))
    kr_rot = _rot_cols(kr.T).T
    pad = jnp.zeros((SMALL_W - SMALL_GATES - g.shape[0], wt.shape[1]), wt.dtype)
    return jnp.concatenate([q, k, v, o, ga, gb, cq, ckv, kr, kr_rot, g, pad], axis=0).astype(jnp.bfloat16)


def _prep_w_uq(w_uq):
    w = w_uq.reshape(Q_LORA, A_HEADS, A_NOPE + A_ROPE)
    rope = w[:, :, A_NOPE:]
    w = jnp.concatenate([w[:, :, :A_NOPE], rope, _rot_cols(rope)], axis=-1)
    return w.reshape(Q_LORA, A_HEADS * Q_HEAD_W).astype(jnp.bfloat16)


def _rope_table(seq):
    inv = 1.0 / (ROPE_THETA ** (jnp.arange(0, A_ROPE, 2, dtype=jnp.float32) / A_ROPE))
    ang = jnp.arange(seq, dtype=jnp.float32)[:, None] * inv[None, :]
    cos, sin = jnp.cos(ang), jnp.sin(ang)
    return jnp.concatenate([cos, cos, sin, sin], axis=-1)


def _route_metadata(ids, wts, T):
    tm = MOE_TILE
    n_tiles = (TOP_K * T) // tm + N_EXPERTS
    P = n_tiles * tm
    eid = jnp.concatenate([ids[:, 0], ids[:, 1]])
    w = jnp.concatenate([wts[:, 0], wts[:, 1]])
    tok = jnp.concatenate([jnp.arange(T, dtype=jnp.int32)] * TOP_K)
    onehot = (eid[:, None] == jnp.arange(N_EXPERTS, dtype=jnp.int32)[None, :]).astype(jnp.int32)
    rank = jnp.sum((jnp.cumsum(onehot, axis=0) - onehot) * onehot, axis=1)
    counts = jnp.sum(onehot, axis=0)
    padded = ((counts + tm - 1) // tm) * tm
    ends = jnp.cumsum(padded)
    starts = ends - padded
    pos = (starts[eid] + rank).astype(jnp.int32)
    src = jnp.zeros((P,), jnp.int32).at[pos].set(tok)
    row_w = jnp.zeros((P,), jnp.float32).at[pos].set(w)
    tile_start = jnp.arange(n_tiles, dtype=jnp.int32) * tm
    tile_e = jnp.minimum(jnp.searchsorted(ends, tile_start, side="right"), N_EXPERTS - 1).astype(jnp.int32)
    tile_v = (tile_start < ends[-1]).astype(jnp.int32)
    return src, row_w.reshape(P, 1), pos, tile_e, tile_v


def kernel(x, ln0_g, ln0_b, w_in, b_gates, conv_qk, head_norm_g, q_norm_g, w_uq, kv_norm_g, w_ukv, w_pm, w_pa,
           w_out, ln1_g, ln1_b, w_rg, b_rg, w_re, b_re, w_e_gate, w_e_up, w_e_down, ln2_g, ln2_b):
    B, S, D = x.shape
    T = B * S
    bf16 = jnp.bfloat16
    cs = _rope_table(S)
    xf, xb = layer_norm_in(x.reshape(T, D), ln0_g, ln0_b)
    for l in range(DEPTH):
        w_in_t = _prep_w_in_t(w_in[l])
        qk = matmul_nt(xb, w_in_t, jnp.float32, COL_QK, 2 * M_QK, name="proj_qk")
        v_o_g = matmul_nt(xb, w_in_t, bf16, COL_V, M_V + M_V + 2 * D_MODEL, name="proj_vog")
        small = matmul_nt(xb, w_in_t, jnp.float32, COL_SMALL, SMALL_W, name="proj_small")

        k_scale = jnp.concatenate([jnp.ones((1, M_QK), jnp.float32),
                                   jnp.full((1, M_QK), M_QK_DIM ** -0.5, jnp.float32)], axis=1)
        qk_c = conv_silu(qk, conv_qk[l], k_scale, S)
        nc = S // MLSTM_CHUNK
        gates = small[:, SMALL_GATES:SMALL_GATES + 4 * M_HEADS].reshape(B, nc, MLSTM_CHUNK, 4, M_HEADS)
        gates = gates.transpose(0, 4, 1, 3, 2)
        bias = b_gates[l].reshape(4, M_HEADS).T.reshape(M_HEADS, 4, 1)
        hm = mlstm(qk_c, v_o_g, gates, bias, head_norm_g[l], B, S)

        cqn, ckvn, krope = mla_prep(small, cs, q_norm_g[l], kv_norm_g[l], S)
        q_cat = q_up(cqn, _prep_w_uq(w_uq[l]), cs, S)
        kv = matmul(ckvn, w_ukv[l].astype(bf16), bf16, name="kv_up")
        att = attention(q_cat, kv, krope, B, S)

        merged = merge(hm, att, w_pm[l].astype(bf16), w_pa[l].astype(bf16), v_o_g)
        mixed = matmul(merged, w_out[l].astype(bf16), jnp.float32, name="proj_out")

        w_r = jnp.concatenate([w_rg[l], w_re[l],
                               jnp.zeros((D, ROUTER_W - N_GROUPS - N_EXPERTS), jnp.float32)], axis=1)
        w_r3 = jnp.stack(_split3(w_r))
        b_r = jnp.concatenate([b_rg[l], b_re[l], jnp.zeros((ROUTER_W - N_GROUPS - N_EXPERTS,), jnp.float32)])
        x1, x1p, ids, wts = ln_router(xf, mixed, ln1_g[l], ln1_b[l], w_r3, b_r.reshape(1, ROUTER_W))

        src, row_w, pos, tile_e, tile_v = _route_metadata(ids, wts, T)
        xs = gather_rows(x1p, src, MOE_TILE)
        h = moe_up(xs, w_e_gate[l], w_e_up[l], tile_e, tile_v)
        y = moe_down(h, w_e_down[l], row_w, tile_e, tile_v)
        ysel = gather_rows(y, pos, 256)
        xf = ln_out(x1, ysel, ln2_g[l], ln2_b[l])
        if l + 1 < DEPTH:
            xb = xf.astype(bf16)
    return xf.reshape(B, S, D)
```

```python
import functools
import math

import jax
import jax.numpy as jnp
import numpy as np
from jax import lax
from jax.experimental import pallas as pl
from jax.experimental.pallas import tpu as pltpu

D_MODEL = 4096
M_HEADS = 8
M_QK_DIM = 256
M_V_DIM = 512
M_QK = M_HEADS * M_QK_DIM
M_V = M_HEADS * M_V_DIM
A_HEADS = 32
A_NOPE = 128
A_ROPE = 64
A_V = 128
Q_LORA = 1024
KV_LORA = 512
ROPE_THETA = 10000.0
ATTN_SCALE = (A_NOPE + A_ROPE) ** -0.5
N_GROUPS = 8
EXPERTS_PER_GROUP = 4
N_EXPERTS = 32
TOP_K = 2
D_EXPERT = 1024
LN_EPS = 1e-5
RMS_EPS = 1e-6
DEPTH = 1
DEEPNORM_ALPHA = (2 * DEPTH) ** 0.25
IN_SIZES = (M_QK, M_QK, M_V, M_V, 4 * M_HEADS, Q_LORA, KV_LORA, A_ROPE, D_MODEL, D_MODEL)

V7X_LANES = 128
V7X_VMEM_LIMIT = 56 * 1024 * 1024

MLSTM_CHUNK = 256
MOE_TILE = 512
SMALL_W = 2048
SMALL_CKV = Q_LORA
SMALL_KR = Q_LORA + KV_LORA
SMALL_GATES = SMALL_KR + 2 * A_ROPE
COL_QK = 0
COL_V = 2 * M_QK
COL_O = COL_V + M_V
COL_GA = COL_O + M_V
COL_GB = COL_GA + D_MODEL
COL_SMALL = COL_GB + D_MODEL
ROUTER_W = 128


def _params(*sem):
    return pltpu.CompilerParams(dimension_semantics=sem, vmem_limit_bytes=V7X_VMEM_LIMIT)


def _tile(n, pref):
    t = min(n, pref)
    while n % t:
        t //= 2
    return t


def _sigmoid(x):
    return 1.0 / (1.0 + jnp.exp(-x))


def _ln_body(z, g_ref, b_ref):
    mu = jnp.mean(z, axis=-1, keepdims=True)
    zc = z - mu
    var = jnp.mean(zc * zc, axis=-1, keepdims=True)
    return zc * lax.rsqrt(var + LN_EPS) * g_ref[...] + b_ref[...]


def _ln_kernel(x_ref, g_ref, b_ref, of_ref, ob_ref):
    y = _ln_body(x_ref[...], g_ref, b_ref)
    of_ref[...] = y
    ob_ref[...] = y.astype(jnp.bfloat16)


def layer_norm_in(x, g, b):
    T, D = x.shape
    tm = _tile(T, 256)
    row = pl.BlockSpec((tm, D), lambda i: (i, 0))
    vec = pl.BlockSpec((1, D), lambda i: (0, 0))
    return pl.pallas_call(
        _ln_kernel, grid=(T // tm,), in_specs=[row, vec, vec], out_specs=[row, row],
        out_shape=[jax.ShapeDtypeStruct((T, D), jnp.float32), jax.ShapeDtypeStruct((T, D), jnp.bfloat16)],
        compiler_params=_params("parallel"), name="ln_in",
    )(x, g.reshape(1, D), b.reshape(1, D))


def _mm_kernel(a_ref, w_ref, o_ref):
    o_ref[...] = jnp.dot(a_ref[...], w_ref[...], preferred_element_type=jnp.float32).astype(o_ref.dtype)


def matmul(a, w, out_dtype, col0=0, ncols=None, tm=1024, tn=1024, name="mm"):
    M, K = a.shape
    ncols = w.shape[1] - col0 if ncols is None else ncols
    tm = _tile(M, tm)
    tn = _tile(ncols, tn)
    assert col0 % tn == 0
    cb = col0 // tn
    return pl.pallas_call(
        _mm_kernel, grid=(ncols // tn, M // tm),
        in_specs=[pl.BlockSpec((tm, K), lambda j, i: (i, 0)),
                  pl.BlockSpec((K, tn), lambda j, i: (0, j + cb))],
        out_specs=pl.BlockSpec((tm, tn), lambda j, i: (i, j)),
        out_shape=jax.ShapeDtypeStruct((M, ncols), out_dtype),
        compiler_params=_params("parallel", "parallel"), name=name,
    )(a, w)


_NT = (((1,), (1,)), ((), ()))


def _mm_nt_kernel(a_ref, w_ref, o_ref):
    o_ref[...] = lax.dot_general(a_ref[...], w_ref[...], _NT,
                                 preferred_element_type=jnp.float32).astype(o_ref.dtype)


def matmul_nt(a, wt, out_dtype, row0, nrows, tm=1024, tn=1024, name="mm_nt"):
    M, K = a.shape
    tm = _tile(M, tm)
    tn = _tile(nrows, tn)
    assert row0 % tn == 0
    rb = row0 // tn
    return pl.pallas_call(
        _mm_nt_kernel, grid=(nrows // tn, M // tm),
        in_specs=[pl.BlockSpec((tm, K), lambda j, i: (i, 0)),
                  pl.BlockSpec((tn, K), lambda j, i: (j + rb, 0))],
        out_specs=pl.BlockSpec((tm, tn), lambda j, i: (i, j)),
        out_shape=jax.ShapeDtypeStruct((M, nrows), out_dtype),
        compiler_params=_params("parallel", "parallel"), name=name,
    )(a, wt)


def _conv_silu_kernel(seq, x_ref, prev_ref, next_ref, w_ref, s_ref, o_ref):
    tm = x_ref.shape[0]
    i = pl.program_id(0)
    x = x_ref[...]
    row = lax.broadcasted_iota(jnp.int32, x.shape, 0)
    first = (i * tm) % seq == 0
    last = ((i + 1) * tm) % seq == 0
    prev_row = jnp.where(first, 0.0, prev_ref[7:8, :])
    next_row = jnp.where(last, 0.0, next_ref[0:1, :])
    x_prev = jnp.where(row == 0, prev_row, pltpu.roll(x, 1, 0))
    x_next = jnp.where(row == tm - 1, next_row, pltpu.roll(x, tm - 1, 0))
    y = w_ref[0:1, :] * x_prev + w_ref[1:2, :] * x + w_ref[2:3, :] * x_next
    y = y * _sigmoid(y) * s_ref[...]
    o_ref[...] = y.astype(o_ref.dtype)


def conv_silu(qk, conv_w, col_scale, seq):
    T, C = qk.shape
    tm = _tile(seq, 512)
    tc = _tile(C, 1024)
    nb8 = T // 8
    return pl.pallas_call(
        functools.partial(_conv_silu_kernel, seq), grid=(T // tm, C // tc),
        in_specs=[pl.BlockSpec((tm, tc), lambda i, j: (i, j)),
                  pl.BlockSpec((8, tc), lambda i, j: (jnp.maximum(i * (tm // 8) - 1, 0), j)),
                  pl.BlockSpec((8, tc), lambda i, j: (jnp.minimum((i + 1) * (tm // 8), nb8 - 1), j)),
                  pl.BlockSpec((3, tc), lambda i, j: (0, j)),
                  pl.BlockSpec((1, tc), lambda i, j: (0, j))],
        out_specs=pl.BlockSpec((tm, tc), lambda i, j: (i, j)),
        out_shape=jax.ShapeDtypeStruct((T, C), jnp.bfloat16),
        compiler_params=_params("parallel", "parallel"), name="conv_silu",
    )(qk, qk, qk, conv_w, col_scale)


def _log_sigmoid(x):
    return jnp.minimum(x, 0.0) - jnp.log(1.0 + jnp.exp(-jnp.abs(x)))


def _mlstm_chunk(q, k, v, gi_row, gf_row, c_ref, n_ref, m_ref, reverse):
    L = q.shape[0]
    t_idx = lax.broadcasted_iota(jnp.int32, (L, L), 0)
    s_idx = lax.broadcasted_iota(jnp.int32, (L, L), 1)
    seen = (s_idx >= t_idx) if reverse else (s_idx <= t_idx)
    eye = s_idx == t_idx
    f_row = _log_sigmoid(gf_row)
    b_col = jnp.sum(jnp.where(seen, f_row, 0.0), axis=1, keepdims=True)
    b_row = jnp.sum(jnp.where(eye, b_col, 0.0), axis=0, keepdims=True)
    i_col = jnp.sum(jnp.where(eye, gi_row, 0.0), axis=1, keepdims=True)
    b_last = jnp.sum(f_row, axis=1, keepdims=True)
    m_prev = m_ref[...]

    dmat = jnp.where(seen, b_col - b_row + gi_row, -jnp.inf)
    inter = b_col + m_prev
    m_t = jnp.maximum(inter, jnp.max(dmat, axis=1, keepdims=True))
    w_inter = jnp.exp(inter - m_t)
    s_qk = lax.dot_general(q, k, (((1,), (1,)), ((), ())), preferred_element_type=jnp.float32)
    s_qk = s_qk * jnp.exp(dmat - m_t)
    c_prev = c_ref[...]
    num = w_inter * jnp.dot(q, c_prev.astype(jnp.bfloat16), preferred_element_type=jnp.float32)
    num = num + jnp.dot(s_qk.astype(jnp.bfloat16), v, preferred_element_type=jnp.float32)
    qn = jnp.sum(q.astype(jnp.float32) * n_ref[...], axis=1, keepdims=True)
    den = w_inter * qn + jnp.sum(s_qk, axis=1, keepdims=True)
    h = num / jnp.maximum(jnp.abs(den), jnp.exp(-m_t))

    dec_col = b_last - b_col + i_col
    m_new = jnp.maximum(b_last + m_prev, jnp.max(dec_col, axis=0, keepdims=True))
    a = jnp.exp(b_last + m_prev - m_new)
    kw = k.astype(jnp.float32) * jnp.exp(dec_col - m_new)
    c_ref[...] = a * c_prev + lax.dot_general(
        kw.astype(jnp.bfloat16), v, (((0,), (0,)), ((), ())), preferred_element_type=jnp.float32)
    n_ref[...] = a * n_ref[...] + jnp.sum(kw, axis=0, keepdims=True)
    m_ref[...] = m_new
    return h


def _mlstm_kernel(q_ref, k_ref, v_ref, g_ref, bias_ref, o_ref, hg_ref, out_ref, hf_ref, c_ref, n_ref, m_ref):
    L = MLSTM_CHUNK
    nc = q_ref.shape[0] // L

    def reset():
        c_ref[...] = jnp.zeros_like(c_ref)
        n_ref[...] = jnp.zeros_like(n_ref)
        m_ref[...] = jnp.zeros_like(m_ref)

    def chunk(c, reverse):
        r0 = pl.multiple_of(c * L, L)
        rows = pl.ds(r0, L)
        g = g_ref[0, 0, c] + bias_ref[0]
        gi, gf = (g[2:3], g[3:4]) if reverse else (g[0:1], g[1:2])
        h = _mlstm_chunk(q_ref[rows, :], k_ref[rows, :], v_ref[rows, :], gi, gf, c_ref, n_ref, m_ref, reverse)
        return rows, h

    reset()

    def fwd(c, carry):
        rows, h = chunk(c, False)
        hf_ref[rows, :] = h
        return carry

    lax.fori_loop(0, nc, fwd, 0)
    reset()

    def bwd(j, carry):
        rows, h = chunk(nc - 1 - j, True)
        h = h + hf_ref[rows, :]
        h = h * lax.rsqrt(jnp.mean(h * h, axis=-1, keepdims=True) + RMS_EPS) * hg_ref[...]
        out_ref[rows, :] = (_sigmoid(o_ref[rows, :].astype(jnp.float32)) * h).astype(out_ref.dtype)
        return carry

    lax.fori_loop(0, nc, bwd, 0)


def mlstm(qk_c, vog, gates, bias, head_norm_g, batch, seq):
    T = batch * seq
    L = MLSTM_CHUNK
    nc = seq // L
    return pl.pallas_call(
        _mlstm_kernel, grid=(batch, M_HEADS),
        in_specs=[pl.BlockSpec((seq, M_QK_DIM), lambda b, h: (b, h)),
                  pl.BlockSpec((seq, M_QK_DIM), lambda b, h: (b, M_HEADS + h)),
                  pl.BlockSpec((seq, M_V_DIM), lambda b, h: (b, h)),
                  pl.BlockSpec((1, 1, nc, 4, L), lambda b, h: (b, h, 0, 0, 0)),
                  pl.BlockSpec((1, 4, 1), lambda b, h: (h, 0, 0)),
                  pl.BlockSpec((seq, M_V_DIM), lambda b, h: (b, M_HEADS + h)),
                  pl.BlockSpec((1, M_V_DIM), lambda b, h: (0, h))],
        out_specs=pl.BlockSpec((seq, M_V_DIM), lambda b, h: (b, h)),
        out_shape=jax.ShapeDtypeStruct((T, M_V), jnp.bfloat16),
        scratch_shapes=[pltpu.VMEM((seq, M_V_DIM), jnp.float32),
                        pltpu.VMEM((M_QK_DIM, M_V_DIM), jnp.float32),
                        pltpu.VMEM((1, M_QK_DIM), jnp.float32),
                        pltpu.VMEM((1, 1), jnp.float32)],
        compiler_params=_params("parallel", "parallel"), name="mlstm",
    )(qk_c, qk_c, vog, gates, bias, vog, head_norm_g.reshape(1, M_V))


def _rms(x, g_ref):
    return x * lax.rsqrt(jnp.mean(x * x, axis=-1, keepdims=True) + RMS_EPS) * g_ref[...]


def _rope_tile(t, cs):
    y = t * cs
    y = y + pltpu.roll(y, A_ROPE, 1)
    lane = lax.broadcasted_iota(jnp.int32, y.shape, 1)
    return jnp.where(lane < A_ROPE, y, 0.0)


def _mla_prep_kernel(cq_ref, ckv_ref, kr_ref, cs_ref, qg_ref, kvg_ref, cqn_ref, ckvn_ref, krope_ref):
    cqn_ref[...] = _rms(cq_ref[...], qg_ref).astype(jnp.bfloat16)
    ckvn_ref[...] = _rms(ckv_ref[...], kvg_ref).astype(jnp.bfloat16)
    krope_ref[...] = _rope_tile(kr_ref[...], cs_ref[...]).astype(jnp.bfloat16)


def mla_prep(small, cs, q_norm_g, kv_norm_g, seq):
    T = small.shape[0]
    tm = _tile(seq, 512)
    ns = seq // tm
    return pl.pallas_call(
        _mla_prep_kernel, grid=(T // tm,),
        in_specs=[pl.BlockSpec((tm, Q_LORA), lambda i: (i, 0)),
                  pl.BlockSpec((tm, KV_LORA), lambda i: (i, SMALL_CKV // KV_LORA)),
                  pl.BlockSpec((tm, V7X_LANES), lambda i: (i, SMALL_KR // V7X_LANES)),
                  pl.BlockSpec((tm, V7X_LANES), lambda i: (i % ns, 0)),
                  pl.BlockSpec((1, Q_LORA), lambda i: (0, 0)),
                  pl.BlockSpec((1, KV_LORA), lambda i: (0, 0))],
        out_specs=[pl.BlockSpec((tm, Q_LORA), lambda i: (i, 0)),
                   pl.BlockSpec((tm, KV_LORA), lambda i: (i, 0)),
                   pl.BlockSpec((tm, V7X_LANES), lambda i: (i, 0))],
        out_shape=[jax.ShapeDtypeStruct((T, Q_LORA), jnp.bfloat16),
                   jax.ShapeDtypeStruct((T, KV_LORA), jnp.bfloat16),
                   jax.ShapeDtypeStruct((T, V7X_LANES), jnp.bfloat16)],
        compiler_params=_params("parallel"), name="mla_prep",
    )(small, small, small, cs, q_norm_g.reshape(1, Q_LORA), kv_norm_g.reshape(1, KV_LORA))


Q_HEAD_W = 2 * V7X_LANES
Q_SCALE = ATTN_SCALE * math.log2(math.e)


def _q_up_kernel(a_ref, w_ref, cs_ref, o_ref):
    acc = jnp.dot(a_ref[...], w_ref[...], preferred_element_type=jnp.float32)
    cs = cs_ref[...]
    for hh in range(acc.shape[1] // Q_HEAD_W):
        c0 = hh * Q_HEAD_W
        o_ref[:, c0:c0 + A_NOPE] = (acc[:, c0:c0 + A_NOPE] * Q_SCALE).astype(o_ref.dtype)
        r = _rope_tile(acc[:, c0 + A_NOPE:c0 + Q_HEAD_W], cs)
        o_ref[:, c0 + A_NOPE:c0 + Q_HEAD_W] = (r * Q_SCALE).astype(o_ref.dtype)


def q_up(cqn, w_uq_r, cs, seq):
    T, K = cqn.shape
    N = w_uq_r.shape[1]
    tm = _tile(seq, 1024)
    tn = 1024
    ns = seq // tm
    return pl.pallas_call(
        _q_up_kernel, grid=(N // tn, T // tm),
        in_specs=[pl.BlockSpec((tm, K), lambda j, i: (i, 0)),
                  pl.BlockSpec((K, tn), lambda j, i: (0, j)),
                  pl.BlockSpec((tm, V7X_LANES), lambda j, i: (i % ns, 0))],
        out_specs=pl.BlockSpec((tm, tn), lambda j, i: (i, j)),
        out_shape=jax.ShapeDtypeStruct((T, N), jnp.bfloat16),
        compiler_params=_params("parallel", "parallel"), name="q_up",
    )(cqn, w_uq_r, cs)


ATTN_Q_TILE = 4096
ATTN_KV_CHUNK = 512


def _attn_kernel(q_ref, kn_ref, kr_ref, v_ref, o_ref, kcat_ref, vcat_ref):
    @pl.when(pl.program_id(2) == 0)
    def _():
        kcat_ref[:, :A_NOPE] = kn_ref[...]
        kcat_ref[:, A_NOPE:] = kr_ref[...]
        vcat_ref[:, :A_V] = v_ref[...]
        vcat_ref[:, A_V:] = jnp.ones((vcat_ref.shape[0], vcat_ref.shape[1] - A_V), vcat_ref.dtype)

    q = q_ref[...]
    seq = kcat_ref.shape[0]
    ck = min(ATTN_KV_CHUNK, seq)
    m = jnp.full((q.shape[0], 1), -jnp.inf, jnp.float32)
    acc = jnp.zeros((q.shape[0], vcat_ref.shape[1]), jnp.float32)
    for c in range(seq // ck):
        rows = slice(c * ck, (c + 1) * ck)
        s = lax.dot_general(q, kcat_ref[rows, :], _NT, preferred_element_type=jnp.float32)
        m_new = jnp.maximum(m, jnp.max(s, axis=-1, keepdims=True))
        p = jnp.exp2(s - m_new).astype(jnp.bfloat16)
        acc = jnp.exp2(m - m_new) * acc + jnp.dot(p, vcat_ref[rows, :], preferred_element_type=jnp.float32)
        m = m_new
    o_ref[...] = (acc[:, :A_V] / acc[:, A_V:]).astype(o_ref.dtype)


def attention(q_cat, kv, krope, batch, seq):
    T = batch * seq
    tq = _tile(seq, ATTN_Q_TILE)
    nq = seq // tq
    return pl.pallas_call(
        _attn_kernel, grid=(batch, A_HEADS, nq),
        in_specs=[pl.BlockSpec((tq, Q_HEAD_W), lambda b, h, i: (b * nq + i, h)),
                  pl.BlockSpec((seq, A_NOPE), lambda b, h, i: (b, 2 * h)),
                  pl.BlockSpec((seq, V7X_LANES), lambda b, h, i: (b, 0)),
                  pl.BlockSpec((seq, A_V), lambda b, h, i: (b, 2 * h + 1))],
        out_specs=pl.BlockSpec((tq, A_V), lambda b, h, i: (b * nq + i, h)),
        out_shape=jax.ShapeDtypeStruct((T, A_HEADS * A_V), jnp.bfloat16),
        scratch_shapes=[pltpu.VMEM((seq, 2 * V7X_LANES), jnp.bfloat16),
                        pltpu.VMEM((seq, 2 * A_V), jnp.bfloat16)],
        compiler_params=_params("parallel", "parallel", "arbitrary"), name="attention",
    )(q_cat, kv, krope, kv)


def _merge_kernel(a_ref, b_ref, wa_ref, wb_ref, ga_ref, gb_ref, o_ref):
    ya = jnp.dot(a_ref[...], wa_ref[...], preferred_element_type=jnp.float32)
    yb = jnp.dot(b_ref[...], wb_ref[...], preferred_element_type=jnp.float32)
    y = _sigmoid(ga_ref[...].astype(jnp.float32)) * ya + _sigmoid(gb_ref[...].astype(jnp.float32)) * yb
    o_ref[...] = y.astype(o_ref.dtype)


def merge(hm, att, w_pm, w_pa, gates, tm=512, tn=512):
    T, K = hm.shape
    N = w_pm.shape[1]
    tm = _tile(T, tm)
    nb = N // tn
    a_spec = pl.BlockSpec((tm, K), lambda j, i: (i, 0))
    w_spec = pl.BlockSpec((K, tn), lambda j, i: (0, j))
    return pl.pallas_call(
        _merge_kernel, grid=(nb, T // tm),
        in_specs=[a_spec, a_spec, w_spec, w_spec,
                  pl.BlockSpec((tm, tn), lambda j, i: (i, 2 * nb + j)),
                  pl.BlockSpec((tm, tn), lambda j, i: (i, 3 * nb + j))],
        out_specs=pl.BlockSpec((tm, tn), lambda j, i: (i, j)),
        out_shape=jax.ShapeDtypeStruct((T, N), jnp.bfloat16),
        compiler_params=_params("parallel", "parallel"), name="merge",
    )(hm, att, w_pm, w_pa, gates, gates)


def _split3(x):
    hi = x.astype(jnp.bfloat16)
    r = x - hi.astype(jnp.float32)
    mid = r.astype(jnp.bfloat16)
    lo = (r - mid.astype(jnp.float32)).astype(jnp.bfloat16)
    return hi, mid, lo


_HI16 = 0xFFFF0000


def _pack_bf16_halves(xb):
    bits = pltpu.bitcast(xb.astype(jnp.float32), jnp.uint32)
    half = xb.shape[1] // 2
    return (bits[:, :half] >> 16) | (bits[:, half:] & jnp.uint32(_HI16))


def _unpack_bf16_halves(u):
    lo = pltpu.bitcast(u << 16, jnp.float32).astype(jnp.bfloat16)
    hi = pltpu.bitcast(u & jnp.uint32(_HI16), jnp.float32).astype(jnp.bfloat16)
    return lo, hi


def _ln_router_kernel(x_ref, y_ref, g_ref, b_ref, w_ref, rb_ref, xf_ref, xp_ref, ids_ref, wts_ref):
    x1 = _ln_body(DEEPNORM_ALPHA * x_ref[...] + y_ref[...], g_ref, b_ref)
    xf_ref[...] = x1
    xh, xm, xl = _split3(x1)
    xp_ref[...] = _pack_bf16_halves(xh)
    wh, wm, wl = w_ref[0], w_ref[1], w_ref[2]
    dot = functools.partial(jnp.dot, preferred_element_type=jnp.float32)
    logits = (dot(xl, wh) + dot(xm, wm) + dot(xh, wl)) + (dot(xm, wh) + dot(xh, wm)) + dot(xh, wh)
    logits = logits + rb_ref[...]
    lane = lax.broadcasted_iota(jnp.int32, logits.shape, 1)
    neg = -jnp.inf
    gl = jnp.where(lane < N_GROUPS, logits, neg)
    ge = jnp.exp(gl - jnp.max(gl, axis=-1, keepdims=True))
    g_prob = ge / jnp.sum(ge, axis=-1, keepdims=True)
    g_top = jnp.max(g_prob, axis=-1, keepdims=True)
    g_idx = jnp.min(jnp.where(g_prob == g_top, lane, ROUTER_W), axis=-1, keepdims=True)
    e_lo = N_GROUPS + EXPERTS_PER_GROUP * g_idx
    e_in = jnp.where((lane >= e_lo) & (lane < e_lo + EXPERTS_PER_GROUP), logits, neg)
    v1 = jnp.max(e_in, axis=-1, keepdims=True)
    i1 = jnp.min(jnp.where(e_in == v1, lane, ROUTER_W), axis=-1, keepdims=True)
    e_in2 = jnp.where(lane == i1, neg, e_in)
    v2 = jnp.max(e_in2, axis=-1, keepdims=True)
    i2 = jnp.min(jnp.where(e_in2 == v2, lane, ROUTER_W), axis=-1, keepdims=True)
    e2 = jnp.exp(v2 - v1)
    w1 = 1.0 / (1.0 + e2) * g_top
    w2 = e2 / (1.0 + e2) * g_top
    ids_ref[...] = jnp.where(lane == 0, i1 - N_GROUPS, jnp.where(lane == 1, i2 - N_GROUPS, 0))
    wts_ref[...] = jnp.where(lane == 0, w1, jnp.where(lane == 1, w2, 0.0))


def ln_router(x, y, g, b, w_r3, b_r):
    T, D = x.shape
    tm = _tile(T, 256)
    row = pl.BlockSpec((tm, D), lambda i: (i, 0))
    vec = pl.BlockSpec((1, D), lambda i: (0, 0))
    small = pl.BlockSpec((tm, ROUTER_W), lambda i: (i, 0))
    return pl.pallas_call(
        _ln_router_kernel, grid=(T // tm,),
        in_specs=[row, row, vec, vec,
                  pl.BlockSpec((3, D, ROUTER_W), lambda i: (0, 0, 0)),
                  pl.BlockSpec((1, ROUTER_W), lambda i: (0, 0))],
        out_specs=[row, pl.BlockSpec((tm, D // 2), lambda i: (i, 0)), small, small],
        out_shape=[jax.ShapeDtypeStruct((T, D), jnp.float32),
                   jax.ShapeDtypeStruct((T, D // 2), jnp.uint32),
                   jax.ShapeDtypeStruct((T, ROUTER_W), jnp.int32),
                   jax.ShapeDtypeStruct((T, ROUTER_W), jnp.float32)],
        compiler_params=_params("parallel"), name="ln_router",
    )(x, y, g.reshape(1, D), b.reshape(1, D), w_r3, b_r)


GATHER_UNROLL = 8
DMA_PRIORITIES = 2


def _gather_rows_into(idx_ref, base, src_ref, dst_ref, sem):
    rows = dst_ref.shape[0]

    def row_copy(r, src_row):
        return pltpu.make_async_copy(src_ref.at[pl.ds(src_row, 1), :], dst_ref.at[pl.ds(r, 1), :], sem)

    def start(g, carry):
        for k in range(GATHER_UNROLL):
            r = g * GATHER_UNROLL + k
            row_copy(r, idx_ref[base + r]).start(priority=k % DMA_PRIORITIES)
        return carry

    def wait(r, carry):
        row_copy(r, 0).wait()
        return carry

    lax.fori_loop(0, rows // GATHER_UNROLL, start, 0)
    lax.fori_loop(0, rows, wait, 0, unroll=GATHER_UNROLL)


def _gather_kernel(idx_ref, src_ref, o_ref, sem):
    _gather_rows_into(idx_ref, pl.program_id(0) * o_ref.shape[0], src_ref, o_ref, sem)


def _gather_unpack_kernel(idx_ref, src_ref, o_ref, buf_ref, sem):
    _gather_rows_into(idx_ref, pl.program_id(0) * o_ref.shape[0], src_ref, buf_ref, sem)
    lo, hi = _unpack_bf16_halves(buf_ref[...])
    half = lo.shape[1]
    o_ref[:, :half] = lo
    o_ref[:, half:] = hi


def gather_rows(src, idx, rows, unpack=False):
    R = idx.shape[0]
    D = src.shape[1]
    assert R % rows == 0 and rows % GATHER_UNROLL == 0
    out_d, out_dtype = (2 * D, jnp.bfloat16) if unpack else (D, src.dtype)
    scratch = [pltpu.VMEM((rows, D), src.dtype)] if unpack else []
    return pl.pallas_call(
        _gather_unpack_kernel if unpack else _gather_kernel,
        grid_spec=pltpu.PrefetchScalarGridSpec(
            num_scalar_prefetch=1, grid=(R // rows,),
            in_specs=[pl.BlockSpec(memory_space=pl.ANY)],
            out_specs=pl.BlockSpec((rows, out_d), lambda i, idx: (i, 0)),
            scratch_shapes=scratch + [pltpu.SemaphoreType.DMA(())]),
        out_shape=jax.ShapeDtypeStruct((R, out_d), out_dtype),
        compiler_params=_params("arbitrary"), name="gather_rows",
    )(idx, src)


def _expert_changed(te_ref, i):
    return (i == 0) | (te_ref[i] != te_ref[jnp.maximum(i - 1, 0)])


def _moe_up_kernel(te_ref, tv_ref, x_ref, wg_ref, wu_ref, h_ref, wgb_ref, wub_ref):
    i = pl.program_id(1)

    @pl.when(_expert_changed(te_ref, i))
    def _():
        wgb_ref[...] = wg_ref[0].astype(jnp.bfloat16)
        wub_ref[...] = wu_ref[0].astype(jnp.bfloat16)

    @pl.when(tv_ref[i] == 1)
    def _():
        x = x_ref[...]
        g = jnp.dot(x, wgb_ref[...], preferred_element_type=jnp.float32)
        u = jnp.dot(x, wub_ref[...], preferred_element_type=jnp.float32)
        h_ref[...] = (g * _sigmoid(g) * u).astype(h_ref.dtype)

    @pl.when(tv_ref[i] == 0)
    def _():
        h_ref[...] = jnp.zeros_like(h_ref)


def moe_up(xs, w_gate, w_up, tile_e, tile_v, tn=512):
    P = xs.shape[0]
    tm = MOE_TILE
    _, D, F = w_gate.shape
    w_spec = pl.BlockSpec((1, D, tn), lambda j, i, te, tv: (te[i], 0, j))
    return pl.pallas_call(
        _moe_up_kernel,
        grid_spec=pltpu.PrefetchScalarGridSpec(
            num_scalar_prefetch=2, grid=(F // tn, P // tm),
            in_specs=[pl.BlockSpec((tm, D), lambda j, i, te, tv: (i, 0)), w_spec, w_spec],
            out_specs=pl.BlockSpec((tm, tn), lambda j, i, te, tv: (i, j)),
            scratch_shapes=[pltpu.VMEM((D, tn), jnp.bfloat16), pltpu.VMEM((D, tn), jnp.bfloat16)]),
        out_shape=jax.ShapeDtypeStruct((P, F), jnp.bfloat16),
        compiler_params=_params("arbitrary", "arbitrary"), name="moe_up",
    )(tile_e, tile_v, xs, w_gate, w_up)


def _moe_down_kernel(te_ref, tv_ref, h_ref, wd_ref, y_ref, wdb_ref):
    i = pl.program_id(1)

    @pl.when(_expert_changed(te_ref, i))
    def _():
        wdb_ref[...] = wd_ref[0].astype(jnp.bfloat16)

    @pl.when(tv_ref[i] == 1)
    def _():
        y_ref[...] = jnp.dot(h_ref[...], wdb_ref[...], preferred_element_type=jnp.float32)

    @pl.when(tv_ref[i] == 0)
    def _():
        y_ref[...] = jnp.zeros_like(y_ref)


def moe_down(h, w_down, tile_e, tile_v, tn=2048):
    P, F = h.shape
    tm = MOE_TILE
    D = w_down.shape[2]
    return pl.pallas_call(
        _moe_down_kernel,
        grid_spec=pltpu.PrefetchScalarGridSpec(
            num_scalar_prefetch=2, grid=(D // tn, P // tm),
            in_specs=[pl.BlockSpec((tm, F), lambda j, i, te, tv: (i, 0)),
                      pl.BlockSpec((1, F, tn), lambda j, i, te, tv: (te[i], 0, j))],
            out_specs=pl.BlockSpec((tm, tn), lambda j, i, te, tv: (i, j)),
            scratch_shapes=[pltpu.VMEM((F, tn), jnp.bfloat16)]),
        out_shape=jax.ShapeDtypeStruct((P, D), jnp.float32),
        compiler_params=_params("arbitrary", "arbitrary"), name="moe_down",
    )(tile_e, tile_v, h, w_down)


def _ln_out_kernel(x_ref, y0_ref, y1_ref, w_ref, g_ref, b_ref, o_ref):
    w = w_ref[...]
    moe = w[:, 0:1] * y0_ref[...] + w[:, 1:2] * y1_ref[...]
    o_ref[...] = _ln_body(DEEPNORM_ALPHA * x_ref[...] + moe, g_ref, b_ref)


def ln_out(x, ysel, wts, g, b):
    T, D = x.shape
    tm = _tile(T, 256)
    nb = T // tm
    row = pl.BlockSpec((tm, D), lambda i: (i, 0))
    vec = pl.BlockSpec((1, D), lambda i: (0, 0))
    return pl.pallas_call(
        _ln_out_kernel, grid=(nb,),
        in_specs=[row, row, pl.BlockSpec((tm, D), lambda i: (i + nb, 0)),
                  pl.BlockSpec((tm, ROUTER_W), lambda i: (i, 0)), vec, vec],
        out_specs=row, out_shape=jax.ShapeDtypeStruct((T, D), jnp.float32),
        compiler_params=_params("parallel"), name="ln_out",
    )(x, ysel, ysel, wts, g.reshape(1, D), b.reshape(1, D))


def _rot_cols(w):
    half = w.shape[-1] // 2
    return jnp.concatenate([-w[..., half:], w[..., :half]], axis=-1)


def _prep_w_in_t(w_in):
    wt = w_in.T
    offs = np.cumsum((0,) + IN_SIZES)
    q, k, v, o, g, cq, ckv, kr, ga, gb = (wt[offs[n]:offs[n + 1]] for n in range(len(IN_SIZES)))
    kr_rot = _rot_cols(kr.T).T
    pad = jnp.zeros((SMALL_W - SMALL_GATES - g.shape[0], wt.shape[1]), wt.dtype)
    return jnp.concatenate([q, k, v, o, ga, gb, cq, ckv, kr, kr_rot, g, pad], axis=0).astype(jnp.bfloat16)


def _prep_w_uq(w_uq):
    w = w_uq.reshape(Q_LORA, A_HEADS, A_NOPE + A_ROPE)
    rope = w[:, :, A_NOPE:]
    w = jnp.concatenate([w[:, :, :A_NOPE], rope, _rot_cols(rope)], axis=-1)
    return w.reshape(Q_LORA, A_HEADS * Q_HEAD_W).astype(jnp.bfloat16)


def _rope_table(seq):
    inv = 1.0 / (ROPE_THETA ** (jnp.arange(0, A_ROPE, 2, dtype=jnp.float32) / A_ROPE))
    ang = jnp.arange(seq, dtype=jnp.float32)[:, None] * inv[None, :]
    cos, sin = jnp.cos(ang), jnp.sin(ang)
    return jnp.concatenate([cos, cos, sin, sin], axis=-1)


def _route_metadata(ids, T):
    tm = MOE_TILE
    n_tiles = (TOP_K * T) // tm + N_EXPERTS
    P = n_tiles * tm
    eid = jnp.concatenate([ids[:, 0], ids[:, 1]])
    tok = jnp.concatenate([jnp.arange(T, dtype=jnp.int32)] * TOP_K)
    onehot = (eid[:, None] == jnp.arange(N_EXPERTS, dtype=jnp.int32)[None, :]).astype(jnp.int32)
    rank = jnp.sum((jnp.cumsum(onehot, axis=0) - onehot) * onehot, axis=1)
    counts = jnp.sum(onehot, axis=0)
    padded = ((counts + tm - 1) // tm) * tm
    ends = jnp.cumsum(padded)
    starts = ends - padded
    pos = (starts[eid] + rank).astype(jnp.int32)
    src = (jnp.arange(P, dtype=jnp.int32) % T).at[pos].set(tok)
    tile_start = jnp.arange(n_tiles, dtype=jnp.int32) * tm
    tile_e = jnp.minimum(jnp.searchsorted(ends, tile_start, side="right"), N_EXPERTS - 1).astype(jnp.int32)
    tile_v = (tile_start < ends[-1]).astype(jnp.int32)
    return src, pos, tile_e, tile_v


def kernel(x, ln0_g, ln0_b, w_in, b_gates, conv_qk, head_norm_g, q_norm_g, w_uq, kv_norm_g, w_ukv, w_pm, w_pa,
           w_out, ln1_g, ln1_b, w_rg, b_rg, w_re, b_re, w_e_gate, w_e_up, w_e_down, ln2_g, ln2_b):
    B, S, D = x.shape
    T = B * S
    bf16 = jnp.bfloat16
    cs = _rope_table(S)
    xf, xb = layer_norm_in(x.reshape(T, D), ln0_g, ln0_b)
    for l in range(DEPTH):
        w_in_t = _prep_w_in_t(w_in[l])
        qk = matmul_nt(xb, w_in_t, jnp.float32, COL_QK, 2 * M_QK, name="proj_qk")
        v_o_g = matmul_nt(xb, w_in_t, bf16, COL_V, M_V + M_V + 2 * D_MODEL, name="proj_vog")
        small = matmul_nt(xb, w_in_t, jnp.float32, COL_SMALL, SMALL_W, name="proj_small")

        k_scale = jnp.concatenate([jnp.ones((1, M_QK), jnp.float32),
                                   jnp.full((1, M_QK), M_QK_DIM ** -0.5, jnp.float32)], axis=1)
        qk_c = conv_silu(qk, conv_qk[l], k_scale, S)
        nc = S // MLSTM_CHUNK
        gates = small[:, SMALL_GATES:SMALL_GATES + 4 * M_HEADS].reshape(B, nc, MLSTM_CHUNK, 4, M_HEADS)
        gates = gates.transpose(0, 4, 1, 3, 2)
        bias = b_gates[l].reshape(4, M_HEADS).T.reshape(M_HEADS, 4, 1)
        hm = mlstm(qk_c, v_o_g, gates, bias, head_norm_g[l], B, S)

        cqn, ckvn, krope = mla_prep(small, cs, q_norm_g[l], kv_norm_g[l], S)
        q_cat = q_up(cqn, _prep_w_uq(w_uq[l]), cs, S)
        kv = matmul(ckvn, w_ukv[l].astype(bf16), bf16, name="kv_up")
        att = attention(q_cat, kv, krope, B, S)

        merged = merge(hm, att, w_pm[l].astype(bf16), w_pa[l].astype(bf16), v_o_g)
        mixed = matmul(merged, w_out[l].astype(bf16), jnp.float32, name="proj_out")

        w_r = jnp.concatenate([w_rg[l], w_re[l],
                               jnp.zeros((D, ROUTER_W - N_GROUPS - N_EXPERTS), jnp.float32)], axis=1)
        w_r3 = jnp.stack(_split3(w_r))
        b_r = jnp.concatenate([b_rg[l], b_re[l], jnp.zeros((ROUTER_W - N_GROUPS - N_EXPERTS,), jnp.float32)])
        x1, x1p, ids, wts = ln_router(xf, mixed, ln1_g[l], ln1_b[l], w_r3, b_r.reshape(1, ROUTER_W))

        src, pos, tile_e, tile_v = _route_metadata(ids, T)
        xs = gather_rows(x1p, src, MOE_TILE, unpack=True)
        h = moe_up(xs, w_e_gate[l], w_e_up[l], tile_e, tile_v)
        y = moe_down(h, w_e_down[l], tile_e, tile_v)
        ysel = gather_rows(y, pos, 256)
        xf = ln_out(x1, ysel, wts, ln2_g[l], ln2_b[l])
        if l + 1 < DEPTH:
            xb = xf.astype(bf16)
    return xf.reshape(B, S, D)
```

```python
import functools
import math

import jax
import jax.numpy as jnp
import numpy as np
from jax import lax
from jax.experimental import pallas as pl
from jax.experimental.pallas import tpu as pltpu

D_MODEL = 4096
M_HEADS = 8
M_QK_DIM = 256
M_V_DIM = 512
M_QK = M_HEADS * M_QK_DIM
M_V = M_HEADS * M_V_DIM
A_HEADS = 32
A_NOPE = 128
A_ROPE = 64
A_V = 128
Q_LORA = 1024
KV_LORA = 512
ROPE_THETA = 10000.0
ATTN_SCALE = (A_NOPE + A_ROPE) ** -0.5
N_GROUPS = 8
EXPERTS_PER_GROUP = 4
N_EXPERTS = 32
TOP_K = 2
D_EXPERT = 1024
LN_EPS = 1e-5
RMS_EPS = 1e-6
DEPTH = 1
DEEPNORM_ALPHA = (2 * DEPTH) ** 0.25
IN_SIZES = (M_QK, M_QK, M_V, M_V, 4 * M_HEADS, Q_LORA, KV_LORA, A_ROPE, D_MODEL, D_MODEL)

V7X_LANES = 128
V7X_VMEM_LIMIT = 56 * 1024 * 1024

MLSTM_CHUNK = 256
MOE_TILE = 512
SMALL_W = 2048
SMALL_CKV = Q_LORA
SMALL_KR = Q_LORA + KV_LORA
SMALL_GATES = SMALL_KR + 2 * A_ROPE
COL_QK = 0
COL_V = 2 * M_QK
COL_O = COL_V + M_V
COL_GA = COL_O + M_V
COL_GB = COL_GA + D_MODEL
COL_SMALL = COL_GB + D_MODEL
ROUTER_W = 128


def _params(*sem):
    return pltpu.CompilerParams(dimension_semantics=sem, vmem_limit_bytes=V7X_VMEM_LIMIT)


def _tile(n, pref):
    t = min(n, pref)
    while n % t:
        t //= 2
    return t


def _sigmoid(x):
    return 1.0 / (1.0 + jnp.exp(-x))


def _ln_body(z, g_ref, b_ref):
    mu = jnp.mean(z, axis=-1, keepdims=True)
    zc = z - mu
    var = jnp.mean(zc * zc, axis=-1, keepdims=True)
    return zc * lax.rsqrt(var + LN_EPS) * g_ref[...] + b_ref[...]


def _ln_kernel(x_ref, g_ref, b_ref, of_ref, ob_ref):
    y = _ln_body(x_ref[...], g_ref, b_ref)
    of_ref[...] = y
    ob_ref[...] = y.astype(jnp.bfloat16)


def layer_norm_in(x, g, b):
    T, D = x.shape
    tm = _tile(T, 256)
    row = pl.BlockSpec((tm, D), lambda i: (i, 0))
    vec = pl.BlockSpec((1, D), lambda i: (0, 0))
    return pl.pallas_call(
        _ln_kernel, grid=(T // tm,), in_specs=[row, vec, vec], out_specs=[row, row],
        out_shape=[jax.ShapeDtypeStruct((T, D), jnp.float32), jax.ShapeDtypeStruct((T, D), jnp.bfloat16)],
        compiler_params=_params("parallel"), name="ln_in",
    )(x, g.reshape(1, D), b.reshape(1, D))


def _mm_kernel(a_ref, w_ref, o_ref):
    o_ref[...] = jnp.dot(a_ref[...], w_ref[...], preferred_element_type=jnp.float32).astype(o_ref.dtype)


def matmul(a, w, out_dtype, col0=0, ncols=None, tm=1024, tn=1024, name="mm"):
    M, K = a.shape
    ncols = w.shape[1] - col0 if ncols is None else ncols
    tm = _tile(M, tm)
    tn = _tile(ncols, tn)
    assert col0 % tn == 0
    cb = col0 // tn
    return pl.pallas_call(
        _mm_kernel, grid=(ncols // tn, M // tm),
        in_specs=[pl.BlockSpec((tm, K), lambda j, i: (i, 0)),
                  pl.BlockSpec((K, tn), lambda j, i: (0, j + cb))],
        out_specs=pl.BlockSpec((tm, tn), lambda j, i: (i, j)),
        out_shape=jax.ShapeDtypeStruct((M, ncols), out_dtype),
        compiler_params=_params("parallel", "parallel"), name=name,
    )(a, w)


_NT = (((1,), (1,)), ((), ()))


def _mm_nt_kernel(a_ref, w_ref, o_ref):
    o_ref[...] = lax.dot_general(a_ref[...], w_ref[...], _NT,
                                 preferred_element_type=jnp.float32).astype(o_ref.dtype)


def matmul_nt(a, wt, out_dtype, row0, nrows, tm=1024, tn=1024, name="mm_nt"):
    M, K = a.shape
    tm = _tile(M, tm)
    tn = _tile(nrows, tn)
    assert row0 % tn == 0
    rb = row0 // tn
    return pl.pallas_call(
        _mm_nt_kernel, grid=(nrows // tn, M // tm),
        in_specs=[pl.BlockSpec((tm, K), lambda j, i: (i, 0)),
                  pl.BlockSpec((tn, K), lambda j, i: (j + rb, 0))],
        out_specs=pl.BlockSpec((tm, tn), lambda j, i: (i, j)),
        out_shape=jax.ShapeDtypeStruct((M, nrows), out_dtype),
        compiler_params=_params("parallel", "parallel"), name=name,
    )(a, wt)


def _conv_silu_kernel(seq, x_ref, prev_ref, next_ref, w_ref, s_ref, o_ref):
    tm = x_ref.shape[0]
    i = pl.program_id(0)
    x = x_ref[...]
    row = lax.broadcasted_iota(jnp.int32, x.shape, 0)
    first = (i * tm) % seq == 0
    last = ((i + 1) * tm) % seq == 0
    prev_row = jnp.where(first, 0.0, prev_ref[7:8, :])
    next_row = jnp.where(last, 0.0, next_ref[0:1, :])
    x_prev = jnp.where(row == 0, prev_row, pltpu.roll(x, 1, 0))
    x_next = jnp.where(row == tm - 1, next_row, pltpu.roll(x, tm - 1, 0))
    y = w_ref[0:1, :] * x_prev + w_ref[1:2, :] * x + w_ref[2:3, :] * x_next
    y = y * _sigmoid(y) * s_ref[...]
    o_ref[...] = y.astype(o_ref.dtype)


def conv_silu(qk, conv_w, col_scale, seq):
    T, C = qk.shape
    tm = _tile(seq, 512)
    tc = _tile(C, 1024)
    nb8 = T // 8
    return pl.pallas_call(
        functools.partial(_conv_silu_kernel, seq), grid=(T // tm, C // tc),
        in_specs=[pl.BlockSpec((tm, tc), lambda i, j: (i, j)),
                  pl.BlockSpec((8, tc), lambda i, j: (jnp.maximum(i * (tm // 8) - 1, 0), j)),
                  pl.BlockSpec((8, tc), lambda i, j: (jnp.minimum((i + 1) * (tm // 8), nb8 - 1), j)),
                  pl.BlockSpec((3, tc), lambda i, j: (0, j)),
                  pl.BlockSpec((1, tc), lambda i, j: (0, j))],
        out_specs=pl.BlockSpec((tm, tc), lambda i, j: (i, j)),
        out_shape=jax.ShapeDtypeStruct((T, C), jnp.bfloat16),
        compiler_params=_params("parallel", "parallel"), name="conv_silu",
    )(qk, qk, qk, conv_w, col_scale)


def _log_sigmoid(x):
    return jnp.minimum(x, 0.0) - jnp.log(1.0 + jnp.exp(-jnp.abs(x)))


def _mlstm_chunk(q, k, v, gi_row, gf_row, c_ref, n_ref, m_ref, reverse):
    L = q.shape[0]
    t_idx = lax.broadcasted_iota(jnp.int32, (L, L), 0)
    s_idx = lax.broadcasted_iota(jnp.int32, (L, L), 1)
    seen = (s_idx >= t_idx) if reverse else (s_idx <= t_idx)
    eye = s_idx == t_idx
    f_row = _log_sigmoid(gf_row)
    b_col = jnp.sum(jnp.where(seen, f_row, 0.0), axis=1, keepdims=True)
    b_row = jnp.sum(jnp.where(eye, b_col, 0.0), axis=0, keepdims=True)
    i_col = jnp.sum(jnp.where(eye, gi_row, 0.0), axis=1, keepdims=True)
    b_last = jnp.sum(f_row, axis=1, keepdims=True)
    m_prev = m_ref[...]

    dmat = jnp.where(seen, b_col - b_row + gi_row, -jnp.inf)
    inter = b_col + m_prev
    m_t = jnp.maximum(inter, jnp.max(dmat, axis=1, keepdims=True))
    w_inter = jnp.exp(inter - m_t)
    s_qk = lax.dot_general(q, k, (((1,), (1,)), ((), ())), preferred_element_type=jnp.float32)
    s_qk = s_qk * jnp.exp(dmat - m_t)
    c_prev = c_ref[...]
    num = w_inter * jnp.dot(q, c_prev.astype(jnp.bfloat16), preferred_element_type=jnp.float32)
    num = num + jnp.dot(s_qk.astype(jnp.bfloat16), v, preferred_element_type=jnp.float32)
    qn = jnp.sum(q.astype(jnp.float32) * n_ref[...], axis=1, keepdims=True)
    den = w_inter * qn + jnp.sum(s_qk, axis=1, keepdims=True)
    h = num / jnp.maximum(jnp.abs(den), jnp.exp(-m_t))

    dec_col = b_last - b_col + i_col
    m_new = jnp.maximum(b_last + m_prev, jnp.max(dec_col, axis=0, keepdims=True))
    a = jnp.exp(b_last + m_prev - m_new)
    kw = k.astype(jnp.float32) * jnp.exp(dec_col - m_new)
    c_ref[...] = a * c_prev + lax.dot_general(
        kw.astype(jnp.bfloat16), v, (((0,), (0,)), ((), ())), preferred_element_type=jnp.float32)
    n_ref[...] = a * n_ref[...] + jnp.sum(kw, axis=0, keepdims=True)
    m_ref[...] = m_new
    return h


def _mlstm_kernel(q_ref, k_ref, v_ref, g_ref, bias_ref, o_ref, hg_ref, out_ref, hf_ref, c_ref, n_ref, m_ref):
    L = MLSTM_CHUNK
    nc = q_ref.shape[0] // L

    def reset():
        c_ref[...] = jnp.zeros_like(c_ref)
        n_ref[...] = jnp.zeros_like(n_ref)
        m_ref[...] = jnp.zeros_like(m_ref)

    def chunk(c, reverse):
        r0 = pl.multiple_of(c * L, L)
        rows = pl.ds(r0, L)
        g = g_ref[0, 0, c] + bias_ref[0]
        gi, gf = (g[2:3], g[3:4]) if reverse else (g[0:1], g[1:2])
        h = _mlstm_chunk(q_ref[rows, :], k_ref[rows, :], v_ref[rows, :], gi, gf, c_ref, n_ref, m_ref, reverse)
        return rows, h

    reset()

    def fwd(c, carry):
        rows, h = chunk(c, False)
        hf_ref[rows, :] = h
        return carry

    lax.fori_loop(0, nc, fwd, 0)
    reset()

    def bwd(j, carry):
        rows, h = chunk(nc - 1 - j, True)
        h = h + hf_ref[rows, :]
        h = h * lax.rsqrt(jnp.mean(h * h, axis=-1, keepdims=True) + RMS_EPS) * hg_ref[...]
        out_ref[rows, :] = (_sigmoid(o_ref[rows, :].astype(jnp.float32)) * h).astype(out_ref.dtype)
        return carry

    lax.fori_loop(0, nc, bwd, 0)


def mlstm(qk_c, vog, gates, bias, head_norm_g, batch, seq):
    T = batch * seq
    L = MLSTM_CHUNK
    nc = seq // L
    return pl.pallas_call(
        _mlstm_kernel, grid=(batch, M_HEADS),
        in_specs=[pl.BlockSpec((seq, M_QK_DIM), lambda b, h: (b, h)),
                  pl.BlockSpec((seq, M_QK_DIM), lambda b, h: (b, M_HEADS + h)),
                  pl.BlockSpec((seq, M_V_DIM), lambda b, h: (b, h)),
                  pl.BlockSpec((1, 1, nc, 4, L), lambda b, h: (b, h, 0, 0, 0)),
                  pl.BlockSpec((1, 4, 1), lambda b, h: (h, 0, 0)),
                  pl.BlockSpec((seq, M_V_DIM), lambda b, h: (b, M_HEADS + h)),
                  pl.BlockSpec((1, M_V_DIM), lambda b, h: (0, h))],
        out_specs=pl.BlockSpec((seq, M_V_DIM), lambda b, h: (b, h)),
        out_shape=jax.ShapeDtypeStruct((T, M_V), jnp.bfloat16),
        scratch_shapes=[pltpu.VMEM((seq, M_V_DIM), jnp.float32),
                        pltpu.VMEM((M_QK_DIM, M_V_DIM), jnp.float32),
                        pltpu.VMEM((1, M_QK_DIM), jnp.float32),
                        pltpu.VMEM((1, 1), jnp.float32)],
        compiler_params=_params("parallel", "parallel"), name="mlstm",
    )(qk_c, qk_c, vog, gates, bias, vog, head_norm_g.reshape(1, M_V))


def _rms(x, g_ref):
    return x * lax.rsqrt(jnp.mean(x * x, axis=-1, keepdims=True) + RMS_EPS) * g_ref[...]


def _rope_tile(t, cs):
    y = t * cs
    y = y + pltpu.roll(y, A_ROPE, 1)
    lane = lax.broadcasted_iota(jnp.int32, y.shape, 1)
    return jnp.where(lane < A_ROPE, y, 0.0)


def _mla_prep_kernel(cq_ref, ckv_ref, kr_ref, cs_ref, qg_ref, kvg_ref, cqn_ref, ckvn_ref, krope_ref):
    cqn_ref[...] = _rms(cq_ref[...], qg_ref).astype(jnp.bfloat16)
    ckvn_ref[...] = _rms(ckv_ref[...], kvg_ref).astype(jnp.bfloat16)
    krope_ref[...] = _rope_tile(kr_ref[...], cs_ref[...]).astype(jnp.bfloat16)


def mla_prep(small, cs, q_norm_g, kv_norm_g, seq):
    T = small.shape[0]
    tm = _tile(seq, 512)
    ns = seq // tm
    return pl.pallas_call(
        _mla_prep_kernel, grid=(T // tm,),
        in_specs=[pl.BlockSpec((tm, Q_LORA), lambda i: (i, 0)),
                  pl.BlockSpec((tm, KV_LORA), lambda i: (i, SMALL_CKV // KV_LORA)),
                  pl.BlockSpec((tm, V7X_LANES), lambda i: (i, SMALL_KR // V7X_LANES)),
                  pl.BlockSpec((tm, V7X_LANES), lambda i: (i % ns, 0)),
                  pl.BlockSpec((1, Q_LORA), lambda i: (0, 0)),
                  pl.BlockSpec((1, KV_LORA), lambda i: (0, 0))],
        out_specs=[pl.BlockSpec((tm, Q_LORA), lambda i: (i, 0)),
                   pl.BlockSpec((tm, KV_LORA), lambda i: (i, 0)),
                   pl.BlockSpec((tm, V7X_LANES), lambda i: (i, 0))],
        out_shape=[jax.ShapeDtypeStruct((T, Q_LORA), jnp.bfloat16),
                   jax.ShapeDtypeStruct((T, KV_LORA), jnp.bfloat16),
                   jax.ShapeDtypeStruct((T, V7X_LANES), jnp.bfloat16)],
        compiler_params=_params("parallel"), name="mla_prep",
    )(small, small, small, cs, q_norm_g.reshape(1, Q_LORA), kv_norm_g.reshape(1, KV_LORA))


Q_HEAD_W = 2 * V7X_LANES
Q_SCALE = ATTN_SCALE * math.log2(math.e)


def _q_up_kernel(a_ref, w_ref, cs_ref, o_ref):
    acc = jnp.dot(a_ref[...], w_ref[...], preferred_element_type=jnp.float32)
    cs = cs_ref[...]
    for hh in range(acc.shape[1] // Q_HEAD_W):
        c0 = hh * Q_HEAD_W
        o_ref[:, c0:c0 + A_NOPE] = (acc[:, c0:c0 + A_NOPE] * Q_SCALE).astype(o_ref.dtype)
        r = _rope_tile(acc[:, c0 + A_NOPE:c0 + Q_HEAD_W], cs)
        o_ref[:, c0 + A_NOPE:c0 + Q_HEAD_W] = (r * Q_SCALE).astype(o_ref.dtype)


def q_up(cqn, w_uq_r, cs, seq):
    T, K = cqn.shape
    N = w_uq_r.shape[1]
    tm = _tile(seq, 1024)
    tn = 1024
    ns = seq // tm
    return pl.pallas_call(
        _q_up_kernel, grid=(N // tn, T // tm),
        in_specs=[pl.BlockSpec((tm, K), lambda j, i: (i, 0)),
                  pl.BlockSpec((K, tn), lambda j, i: (0, j)),
                  pl.BlockSpec((tm, V7X_LANES), lambda j, i: (i % ns, 0))],
        out_specs=pl.BlockSpec((tm, tn), lambda j, i: (i, j)),
        out_shape=jax.ShapeDtypeStruct((T, N), jnp.bfloat16),
        compiler_params=_params("parallel", "parallel"), name="q_up",
    )(cqn, w_uq_r, cs)


ATTN_Q_TILE = 4096
ATTN_KV_CHUNK = 512


def _attn_kernel(q_ref, kn_ref, kr_ref, v_ref, o_ref, kcat_ref, vcat_ref):
    @pl.when(pl.program_id(2) == 0)
    def _():
        kcat_ref[:, :A_NOPE] = kn_ref[...]
        kcat_ref[:, A_NOPE:] = kr_ref[...]
        vcat_ref[:, :A_V] = v_ref[...]
        vcat_ref[:, A_V:] = jnp.ones((vcat_ref.shape[0], vcat_ref.shape[1] - A_V), vcat_ref.dtype)

    q = q_ref[...]
    seq = kcat_ref.shape[0]
    ck = min(ATTN_KV_CHUNK, seq)
    m = jnp.full((q.shape[0], 1), -jnp.inf, jnp.float32)
    acc = jnp.zeros((q.shape[0], vcat_ref.shape[1]), jnp.float32)
    for c in range(seq // ck):
        rows = slice(c * ck, (c + 1) * ck)
        s = lax.dot_general(q, kcat_ref[rows, :], _NT, preferred_element_type=jnp.float32)
        m_new = jnp.maximum(m, jnp.max(s, axis=-1, keepdims=True))
        p = jnp.exp2(s - m_new).astype(jnp.bfloat16)
        acc = jnp.exp2(m - m_new) * acc + jnp.dot(p, vcat_ref[rows, :], preferred_element_type=jnp.float32)
        m = m_new
    o_ref[...] = (acc[:, :A_V] / acc[:, A_V:]).astype(o_ref.dtype)


def attention(q_cat, kv, krope, batch, seq):
    T = batch * seq
    tq = _tile(seq, ATTN_Q_TILE)
    nq = seq // tq
    return pl.pallas_call(
        _attn_kernel, grid=(batch, A_HEADS, nq),
        in_specs=[pl.BlockSpec((tq, Q_HEAD_W), lambda b, h, i: (b * nq + i, h)),
                  pl.BlockSpec((seq, A_NOPE), lambda b, h, i: (b, 2 * h)),
                  pl.BlockSpec((seq, V7X_LANES), lambda b, h, i: (b, 0)),
                  pl.BlockSpec((seq, A_V), lambda b, h, i: (b, 2 * h + 1))],
        out_specs=pl.BlockSpec((tq, A_V), lambda b, h, i: (b * nq + i, h)),
        out_shape=jax.ShapeDtypeStruct((T, A_HEADS * A_V), jnp.bfloat16),
        scratch_shapes=[pltpu.VMEM((seq, 2 * V7X_LANES), jnp.bfloat16),
                        pltpu.VMEM((seq, 2 * A_V), jnp.bfloat16)],
        compiler_params=_params("parallel", "parallel", "arbitrary"), name="attention",
    )(q_cat, kv, krope, kv)


def _merge_kernel(a_ref, b_ref, wa_ref, wb_ref, ga_ref, gb_ref, o_ref):
    ya = jnp.dot(a_ref[...], wa_ref[...], preferred_element_type=jnp.float32)
    yb = jnp.dot(b_ref[...], wb_ref[...], preferred_element_type=jnp.float32)
    y = _sigmoid(ga_ref[...].astype(jnp.float32)) * ya + _sigmoid(gb_ref[...].astype(jnp.float32)) * yb
    o_ref[...] = y.astype(o_ref.dtype)


def merge(hm, att, w_pm, w_pa, gates, tm=512, tn=512):
    T, K = hm.shape
    N = w_pm.shape[1]
    tm = _tile(T, tm)
    nb = N // tn
    a_spec = pl.BlockSpec((tm, K), lambda j, i: (i, 0))
    w_spec = pl.BlockSpec((K, tn), lambda j, i: (0, j))
    return pl.pallas_call(
        _merge_kernel, grid=(nb, T // tm),
        in_specs=[a_spec, a_spec, w_spec, w_spec,
                  pl.BlockSpec((tm, tn), lambda j, i: (i, 2 * nb + j)),
                  pl.BlockSpec((tm, tn), lambda j, i: (i, 3 * nb + j))],
        out_specs=pl.BlockSpec((tm, tn), lambda j, i: (i, j)),
        out_shape=jax.ShapeDtypeStruct((T, N), jnp.bfloat16),
        compiler_params=_params("parallel", "parallel"), name="merge",
    )(hm, att, w_pm, w_pa, gates, gates)


ROUTER_N = N_GROUPS + N_EXPERTS


def _split2(x):
    hi = x.astype(jnp.bfloat16)
    mid = (x - hi.astype(jnp.float32)).astype(jnp.bfloat16)
    return hi, mid


_HI16 = 0xFFFF0000


def _pack_bf16_pair(lo, hi):
    lo_bits = pltpu.bitcast(lo.astype(jnp.float32), jnp.uint32)
    hi_bits = pltpu.bitcast(hi.astype(jnp.float32), jnp.uint32)
    return (lo_bits >> 16) | (hi_bits & jnp.uint32(_HI16))


def _pack_bf16_halves(xb):
    half = xb.shape[1] // 2
    return _pack_bf16_pair(xb[:, :half], xb[:, half:])


def _unpack_halves_f32(u):
    return pltpu.bitcast(u << 16, jnp.float32), pltpu.bitcast(u & jnp.uint32(_HI16), jnp.float32)


def _unpack_bf16_halves(u):
    lo, hi = _unpack_halves_f32(u)
    return lo.astype(jnp.bfloat16), hi.astype(jnp.bfloat16)


def _ln_router_kernel(x_ref, y_ref, g_ref, b_ref, w_ref, rb_ref, xf_ref, xp_ref, ids_ref, wts_ref):
    x1 = _ln_body(DEEPNORM_ALPHA * x_ref[...] + y_ref[...], g_ref, b_ref)
    xf_ref[...] = x1
    xh, xm = _split2(x1)
    xp_ref[...] = _pack_bf16_halves(xh)
    dot = functools.partial(jnp.dot, preferred_element_type=jnp.float32)
    a = dot(xh, w_ref[0])
    logits = (pltpu.roll(a, ROUTER_W - ROUTER_N, 1) + dot(xm, w_ref[1])) + a
    logits = logits + rb_ref[...]
    lane = lax.broadcasted_iota(jnp.int32, logits.shape, 1)
    neg = -jnp.inf
    gl = jnp.where(lane < N_GROUPS, logits, neg)
    ge = jnp.exp(gl - jnp.max(gl, axis=-1, keepdims=True))
    g_prob = ge / jnp.sum(ge, axis=-1, keepdims=True)
    g_top = jnp.max(g_prob, axis=-1, keepdims=True)
    g_idx = jnp.min(jnp.where(g_prob == g_top, lane, ROUTER_W), axis=-1, keepdims=True)
    e_lo = N_GROUPS + EXPERTS_PER_GROUP * g_idx
    e_in = jnp.where((lane >= e_lo) & (lane < e_lo + EXPERTS_PER_GROUP), logits, neg)
    v1 = jnp.max(e_in, axis=-1, keepdims=True)
    i1 = jnp.min(jnp.where(e_in == v1, lane, ROUTER_W), axis=-1, keepdims=True)
    e_in2 = jnp.where(lane == i1, neg, e_in)
    v2 = jnp.max(e_in2, axis=-1, keepdims=True)
    i2 = jnp.min(jnp.where(e_in2 == v2, lane, ROUTER_W), axis=-1, keepdims=True)
    e2 = jnp.exp(v2 - v1)
    w1 = 1.0 / (1.0 + e2) * g_top
    w2 = e2 / (1.0 + e2) * g_top
    ids_ref[...] = jnp.where(lane == 0, i1 - N_GROUPS, jnp.where(lane == 1, i2 - N_GROUPS, 0))
    wts_ref[...] = jnp.where(lane == 0, w1, jnp.where(lane == 1, w2, 0.0))


def ln_router(x, y, g, b, w_r2, b_r):
    T, D = x.shape
    tm = _tile(T, 256)
    row = pl.BlockSpec((tm, D), lambda i: (i, 0))
    vec = pl.BlockSpec((1, D), lambda i: (0, 0))
    small = pl.BlockSpec((tm, ROUTER_W), lambda i: (i, 0))
    return pl.pallas_call(
        _ln_router_kernel, grid=(T // tm,),
        in_specs=[row, row, vec, vec,
                  pl.BlockSpec((2, D, ROUTER_W), lambda i: (0, 0, 0)),
                  pl.BlockSpec((1, ROUTER_W), lambda i: (0, 0))],
        out_specs=[row, pl.BlockSpec((tm, D // 2), lambda i: (i, 0)), small, small],
        out_shape=[jax.ShapeDtypeStruct((T, D), jnp.float32),
                   jax.ShapeDtypeStruct((T, D // 2), jnp.uint32),
                   jax.ShapeDtypeStruct((T, ROUTER_W), jnp.int32),
                   jax.ShapeDtypeStruct((T, ROUTER_W), jnp.float32)],
        compiler_params=_params("parallel"), name="ln_router",
    )(x, y, g.reshape(1, D), b.reshape(1, D), w_r2, b_r)


GATHER_UNROLL = 8
DMA_PRIORITIES = 2


def _row_copy(src_ref, src_row, dst_ref, r, sem):
    return pltpu.make_async_copy(src_ref.at[pl.ds(src_row, 1), :], dst_ref.at[pl.ds(r, 1), :], sem)


def _start_row_gather(idx_ref, base, src_ref, dst_ref, sem):
    def start(g, carry):
        for k in range(GATHER_UNROLL):
            r = g * GATHER_UNROLL + k
            _row_copy(src_ref, idx_ref[base + r], dst_ref, r, sem).start(priority=k % DMA_PRIORITIES)
        return carry

    lax.fori_loop(0, dst_ref.shape[0] // GATHER_UNROLL, start, 0)


def _wait_row_gather(src_ref, dst_ref, sem):
    def wait(r, carry):
        _row_copy(src_ref, 0, dst_ref, r, sem).wait()
        return carry

    lax.fori_loop(0, dst_ref.shape[0], wait, 0, unroll=GATHER_UNROLL)


def _gather_unpack_kernel(idx_ref, tv_ref, src_ref, o_ref, buf_ref, sem):
    i = pl.program_id(0)

    @pl.when(tv_ref[i] == 1)
    def _():
        _start_row_gather(idx_ref, i * o_ref.shape[0], src_ref, buf_ref, sem)
        _wait_row_gather(src_ref, buf_ref, sem)
        lo, hi = _unpack_bf16_halves(buf_ref[...])
        half = lo.shape[1]
        o_ref[:, :half] = lo
        o_ref[:, half:] = hi

    @pl.when(tv_ref[i] == 0)
    def _():
        o_ref[...] = jnp.zeros_like(o_ref)


def gather_expert_rows(src, idx, tile_v):
    R = idx.shape[0]
    D = src.shape[1]
    rows = MOE_TILE
    assert R % rows == 0 and rows % GATHER_UNROLL == 0
    return pl.pallas_call(
        _gather_unpack_kernel,
        grid_spec=pltpu.PrefetchScalarGridSpec(
            num_scalar_prefetch=2, grid=(R // rows,),
            in_specs=[pl.BlockSpec(memory_space=pl.ANY)],
            out_specs=pl.BlockSpec((rows, 2 * D), lambda i, idx, tv: (i, 0)),
            scratch_shapes=[pltpu.VMEM((rows, D), src.dtype), pltpu.SemaphoreType.DMA(())]),
        out_shape=jax.ShapeDtypeStruct((R, 2 * D), jnp.bfloat16),
        compiler_params=_params("arbitrary"), name="gather_rows",
    )(idx, tile_v, src)


def _expert_changed(te_ref, i):
    return (i == 0) | (te_ref[i] != te_ref[jnp.maximum(i - 1, 0)])


def _moe_up_kernel(te_ref, tv_ref, x_ref, wg_ref, wu_ref, h_ref, wgb_ref, wub_ref):
    i = pl.program_id(1)

    @pl.when(_expert_changed(te_ref, i))
    def _():
        wgb_ref[...] = wg_ref[0].astype(jnp.bfloat16)
        wub_ref[...] = wu_ref[0].astype(jnp.bfloat16)

    @pl.when(tv_ref[i] == 1)
    def _():
        x = x_ref[...]
        g = jnp.dot(x, wgb_ref[...], preferred_element_type=jnp.float32)
        u = jnp.dot(x, wub_ref[...], preferred_element_type=jnp.float32)
        h_ref[...] = (g * _sigmoid(g) * u).astype(h_ref.dtype)

    @pl.when(tv_ref[i] == 0)
    def _():
        h_ref[...] = jnp.zeros_like(h_ref)


def moe_up(xs, w_gate, w_up, tile_e, tile_v, tn=512):
    P = xs.shape[0]
    tm = MOE_TILE
    _, D, F = w_gate.shape
    w_spec = pl.BlockSpec((1, D, tn), lambda j, i, te, tv: (te[i], 0, j))
    return pl.pallas_call(
        _moe_up_kernel,
        grid_spec=pltpu.PrefetchScalarGridSpec(
            num_scalar_prefetch=2, grid=(F // tn, P // tm),
            in_specs=[pl.BlockSpec((tm, D), lambda j, i, te, tv: (i, 0)), w_spec, w_spec],
            out_specs=pl.BlockSpec((tm, tn), lambda j, i, te, tv: (i, j)),
            scratch_shapes=[pltpu.VMEM((D, tn), jnp.bfloat16), pltpu.VMEM((D, tn), jnp.bfloat16)]),
        out_shape=jax.ShapeDtypeStruct((P, F), jnp.bfloat16),
        compiler_params=_params("arbitrary", "arbitrary"), name="moe_up",
    )(tile_e, tile_v, xs, w_gate, w_up)


def _moe_down_kernel(te_ref, tv_ref, h_ref, wa_ref, wb_ref, y_ref, wab_ref, wbb_ref):
    i = pl.program_id(1)

    @pl.when(_expert_changed(te_ref, i))
    def _():
        wab_ref[...] = wa_ref[0].astype(jnp.bfloat16)
        wbb_ref[...] = wb_ref[0].astype(jnp.bfloat16)

    @pl.when(tv_ref[i] == 1)
    def _():
        h = h_ref[...]
        ya = jnp.dot(h, wab_ref[...], preferred_element_type=jnp.float32)
        yb = jnp.dot(h, wbb_ref[...], preferred_element_type=jnp.float32)
        y_ref[...] = _pack_bf16_pair(ya.astype(jnp.bfloat16), yb.astype(jnp.bfloat16))

    @pl.when(tv_ref[i] == 0)
    def _():
        y_ref[...] = jnp.zeros_like(y_ref)


def moe_down(h, w_down, tile_e, tile_v, tn=1024):
    P, F = h.shape
    tm = MOE_TILE
    half = w_down.shape[2] // 2
    nj = half // tn
    return pl.pallas_call(
        _moe_down_kernel,
        grid_spec=pltpu.PrefetchScalarGridSpec(
            num_scalar_prefetch=2, grid=(nj, P // tm),
            in_specs=[pl.BlockSpec((tm, F), lambda j, i, te, tv: (i, 0)),
                      pl.BlockSpec((1, F, tn), lambda j, i, te, tv: (te[i], 0, j)),
                      pl.BlockSpec((1, F, tn), lambda j, i, te, tv: (te[i], 0, j + nj))],
            out_specs=pl.BlockSpec((tm, tn), lambda j, i, te, tv: (i, j)),
            scratch_shapes=[pltpu.VMEM((F, tn), jnp.bfloat16), pltpu.VMEM((F, tn), jnp.bfloat16)]),
        out_shape=jax.ShapeDtypeStruct((P, half), jnp.uint32),
        compiler_params=_params("arbitrary", "arbitrary"), name="moe_down",
    )(tile_e, tile_v, h, w_down, w_down)


def _ln_out_kernel(pos_ref, x_ref, w_ref, g_ref, b_ref, y_ref, o_ref, buf_ref, sem):
    i = pl.program_id(0)
    n = pl.num_programs(0)
    tm = x_ref.shape[0]

    def parts(step, slot):
        return [(k * n * tm + step * tm, buf_ref.at[slot, pl.ds(k * tm, tm)]) for k in range(TOP_K)]

    def start(step, slot):
        for base, dst in parts(step, slot):
            _start_row_gather(pos_ref, base, y_ref, dst, sem.at[slot])

    @pl.when(i == 0)
    def _():
        start(0, 0)

    @pl.when(i + 1 < n)
    def _():
        start(i + 1, (i + 1) % 2)

    slot = i % 2
    for _, dst in parts(0, slot):
        _wait_row_gather(y_ref, dst, sem.at[slot])

    yb = buf_ref[slot]
    lo0, hi0 = _unpack_halves_f32(yb[:tm])
    lo1, hi1 = _unpack_halves_f32(yb[tm:])
    w = w_ref[...]
    w0, w1 = w[:, 0:1], w[:, 1:2]
    x = x_ref[...]
    half = lo0.shape[1]
    z = jnp.concatenate([DEEPNORM_ALPHA * x[:, :half] + (w0 * lo0 + w1 * lo1),
                         DEEPNORM_ALPHA * x[:, half:] + (w0 * hi0 + w1 * hi1)], axis=1)
    o_ref[...] = _ln_body(z, g_ref, b_ref)


def ln_out(x, y, pos, wts, g, b):
    T, D = x.shape
    tm = _tile(T, 256)
    row = pl.BlockSpec((tm, D), lambda i, pos: (i, 0))
    vec = pl.BlockSpec((1, D), lambda i, pos: (0, 0))
    return pl.pallas_call(
        _ln_out_kernel,
        grid_spec=pltpu.PrefetchScalarGridSpec(
            num_scalar_prefetch=1, grid=(T // tm,),
            in_specs=[row, pl.BlockSpec((tm, ROUTER_W), lambda i, pos: (i, 0)), vec, vec,
                      pl.BlockSpec(memory_space=pl.ANY)],
            out_specs=row,
            scratch_shapes=[pltpu.VMEM((2, TOP_K * tm, D // 2), jnp.uint32), pltpu.SemaphoreType.DMA((2,))]),
        out_shape=jax.ShapeDtypeStruct((T, D), jnp.float32),
        compiler_params=_params("arbitrary"), name="ln_out",
    )(pos, x, wts, g.reshape(1, D), b.reshape(1, D), y)


def _rot_cols(w):
    half = w.shape[-1] // 2
    return jnp.concatenate([-w[..., half:], w[..., :half]], axis=-1)


def _prep_w_in_t(w_in):
    wt = w_in.T
    offs = np.cumsum((0,) + IN_SIZES)
    q, k, v, o, g, cq, ckv, kr, ga, gb = (wt[offs[n]:offs[n + 1]] for n in range(len(IN_SIZES)))
    kr_rot = _rot_cols(kr.T).T
    pad = jnp.zeros((SMALL_W - SMALL_GATES - g.shape[0], wt.shape[1]), wt.dtype)
    return jnp.concatenate([q, k, v, o, ga, gb, cq, ckv, kr, kr_rot, g, pad], axis=0).astype(jnp.bfloat16)


def _prep_w_uq(w_uq):
    w = w_uq.reshape(Q_LORA, A_HEADS, A_NOPE + A_ROPE)
    rope = w[:, :, A_NOPE:]
    w = jnp.concatenate([w[:, :, :A_NOPE], rope, _rot_cols(rope)], axis=-1)
    return w.reshape(Q_LORA, A_HEADS * Q_HEAD_W).astype(jnp.bfloat16)


def _rope_table(seq):
    inv = 1.0 / (ROPE_THETA ** (jnp.arange(0, A_ROPE, 2, dtype=jnp.float32) / A_ROPE))
    ang = jnp.arange(seq, dtype=jnp.float32)[:, None] * inv[None, :]
    cos, sin = jnp.cos(ang), jnp.sin(ang)
    return jnp.concatenate([cos, cos, sin, sin], axis=-1)


def _route_metadata(ids, T):
    tm = MOE_TILE
    n_tiles = (TOP_K * T) // tm + N_EXPERTS
    P = n_tiles * tm
    eid = jnp.concatenate([ids[:, 0], ids[:, 1]])
    tok = jnp.concatenate([jnp.arange(T, dtype=jnp.int32)] * TOP_K)
    onehot = (eid[:, None] == jnp.arange(N_EXPERTS, dtype=jnp.int32)[None, :]).astype(jnp.int32)
    rank = jnp.sum((jnp.cumsum(onehot, axis=0) - onehot) * onehot, axis=1)
    counts = jnp.sum(onehot, axis=0)
    padded = ((counts + tm - 1) // tm) * tm
    ends = jnp.cumsum(padded)
    starts = ends - padded
    pos = (starts[eid] + rank).astype(jnp.int32)
    src = (jnp.arange(P, dtype=jnp.int32) % T).at[pos].set(tok)
    tile_start = jnp.arange(n_tiles, dtype=jnp.int32) * tm
    tile_e = jnp.minimum(jnp.searchsorted(ends, tile_start, side="right"), N_EXPERTS - 1).astype(jnp.int32)
    tile_v = (tile_start < ends[-1]).astype(jnp.int32)
    return src, pos, tile_e, tile_v


def kernel(x, ln0_g, ln0_b, w_in, b_gates, conv_qk, head_norm_g, q_norm_g, w_uq, kv_norm_g, w_ukv, w_pm, w_pa,
           w_out, ln1_g, ln1_b, w_rg, b_rg, w_re, b_re, w_e_gate, w_e_up, w_e_down, ln2_g, ln2_b):
    B, S, D = x.shape
    T = B * S
    bf16 = jnp.bfloat16
    cs = _rope_table(S)
    xf, xb = layer_norm_in(x.reshape(T, D), ln0_g, ln0_b)
    for l in range(DEPTH):
        w_in_t = _prep_w_in_t(w_in[l])
        qk = matmul_nt(xb, w_in_t, jnp.float32, COL_QK, 2 * M_QK, name="proj_qk")
        v_o_g = matmul_nt(xb, w_in_t, bf16, COL_V, M_V + M_V + 2 * D_MODEL, name="proj_vog")
        small = matmul_nt(xb, w_in_t, jnp.float32, COL_SMALL, SMALL_W, name="proj_small")

        k_scale = jnp.concatenate([jnp.ones((1, M_QK), jnp.float32),
                                   jnp.full((1, M_QK), M_QK_DIM ** -0.5, jnp.float32)], axis=1)
        qk_c = conv_silu(qk, conv_qk[l], k_scale, S)
        nc = S // MLSTM_CHUNK
        gates = small[:, SMALL_GATES:SMALL_GATES + 4 * M_HEADS].reshape(B, nc, MLSTM_CHUNK, 4, M_HEADS)
        gates = gates.transpose(0, 4, 1, 3, 2)
        bias = b_gates[l].reshape(4, M_HEADS).T.reshape(M_HEADS, 4, 1)
        hm = mlstm(qk_c, v_o_g, gates, bias, head_norm_g[l], B, S)

        cqn, ckvn, krope = mla_prep(small, cs, q_norm_g[l], kv_norm_g[l], S)
        q_cat = q_up(cqn, _prep_w_uq(w_uq[l]), cs, S)
        kv = matmul(ckvn, w_ukv[l].astype(bf16), bf16, name="kv_up")
        att = attention(q_cat, kv, krope, B, S)

        merged = merge(hm, att, w_pm[l].astype(bf16), w_pa[l].astype(bf16), v_o_g)
        mixed = matmul(merged, w_out[l].astype(bf16), jnp.float32, name="proj_out")

        w_hi, w_mid = _split2(jnp.concatenate([w_rg[l], w_re[l]], axis=1))
        w_r2 = jnp.stack([jnp.pad(jnp.concatenate([w_hi, w_mid], axis=1), ((0, 0), (0, ROUTER_W - 2 * ROUTER_N))),
                          jnp.pad(w_hi, ((0, 0), (0, ROUTER_W - ROUTER_N)))])
        b_r = jnp.pad(jnp.concatenate([b_rg[l], b_re[l]]), (0, ROUTER_W - ROUTER_N))
        x1, x1p, ids, wts = ln_router(xf, mixed, ln1_g[l], ln1_b[l], w_r2, b_r.reshape(1, ROUTER_W))

        src, pos, tile_e, tile_v = _route_metadata(ids, T)
        xs = gather_expert_rows(x1p, src, tile_v)
        h = moe_up(xs, w_e_gate[l], w_e_up[l], tile_e, tile_v)
        y = moe_down(h, w_e_down[l], tile_e, tile_v)
        xf = ln_out(x1, y, pos, wts, ln2_g[l], ln2_b[l])
        if l + 1 < DEPTH:
            xb = xf.astype(bf16)
    return xf.reshape(B, S, D)
```

```python
import functools
import math

import jax
import jax.numpy as jnp
import numpy as np
from jax import lax
from jax.experimental import pallas as pl
from jax.experimental.pallas import tpu as pltpu

D_MODEL = 4096
M_HEADS = 8
M_QK_DIM = 256
M_V_DIM = 512
M_QK = M_HEADS * M_QK_DIM
M_V = M_HEADS * M_V_DIM
A_HEADS = 32
A_NOPE = 128
A_ROPE = 64
A_V = 128
Q_LORA = 1024
KV_LORA = 512
ROPE_THETA = 10000.0
ATTN_SCALE = (A_NOPE + A_ROPE) ** -0.5
N_GROUPS = 8
EXPERTS_PER_GROUP = 4
N_EXPERTS = 32
TOP_K = 2
D_EXPERT = 1024
LN_EPS = 1e-5
RMS_EPS = 1e-6
DEPTH = 1
DEEPNORM_ALPHA = (2 * DEPTH) ** 0.25
IN_SIZES = (M_QK, M_QK, M_V, M_V, 4 * M_HEADS, Q_LORA, KV_LORA, A_ROPE, D_MODEL, D_MODEL)

V7X_LANES = 128
BF16_SUBLANES = 16
V7X_VMEM_LIMIT = 56 * 1024 * 1024

MLSTM_CHUNK = 256
MOE_TILE = 512
SMALL_W = 2048
SMALL_CKV = Q_LORA
SMALL_KR = Q_LORA + KV_LORA
SMALL_GATES = SMALL_KR + 2 * A_ROPE
ROUTER_W = 128


def _params(*sem):
    return pltpu.CompilerParams(dimension_semantics=sem, vmem_limit_bytes=V7X_VMEM_LIMIT)


def _tile(n, pref):
    t = min(n, pref)
    while n % t:
        t //= 2
    return t


def _sigmoid(x):
    return 1.0 / (1.0 + jnp.exp(-x))


def _ln_body(z, g_ref, b_ref):
    mu = jnp.mean(z, axis=-1, keepdims=True)
    zc = z - mu
    var = jnp.mean(zc * zc, axis=-1, keepdims=True)
    return zc * lax.rsqrt(var + LN_EPS) * g_ref[...] + b_ref[...]


def _ln_kernel(x_ref, g_ref, b_ref, of_ref, ob_ref):
    y = _ln_body(x_ref[...], g_ref, b_ref)
    of_ref[...] = y
    ob_ref[...] = y.astype(jnp.bfloat16)


def layer_norm_in(x, g, b):
    T, D = x.shape
    tm = _tile(T, 256)
    row = pl.BlockSpec((tm, D), lambda i: (i, 0))
    vec = pl.BlockSpec((1, D), lambda i: (0, 0))
    return pl.pallas_call(
        _ln_kernel, grid=(T // tm,), in_specs=[row, vec, vec], out_specs=[row, row],
        out_shape=[jax.ShapeDtypeStruct((T, D), jnp.float32), jax.ShapeDtypeStruct((T, D), jnp.bfloat16)],
        compiler_params=_params("parallel"), name="ln_in",
    )(x, g.reshape(1, D), b.reshape(1, D))


def _mm_kernel(a_ref, w_ref, o_ref):
    o_ref[...] = jnp.dot(a_ref[...], w_ref[...], preferred_element_type=jnp.float32).astype(o_ref.dtype)


def matmul(a, w, out_dtype, col0=0, ncols=None, tm=1024, tn=1024, name="mm"):
    M, K = a.shape
    ncols = w.shape[1] - col0 if ncols is None else ncols
    tm = _tile(M, tm)
    tn = _tile(ncols, tn)
    assert col0 % tn == 0
    cb = col0 // tn
    return pl.pallas_call(
        _mm_kernel, grid=(ncols // tn, M // tm),
        in_specs=[pl.BlockSpec((tm, K), lambda j, i: (i, 0)),
                  pl.BlockSpec((K, tn), lambda j, i: (0, j + cb))],
        out_specs=pl.BlockSpec((tm, tn), lambda j, i: (i, j)),
        out_shape=jax.ShapeDtypeStruct((M, ncols), out_dtype),
        compiler_params=_params("parallel", "parallel"), name=name,
    )(a, w)


_NT = (((1,), (1,)), ((), ()))


def _mm_nt_kernel(a_ref, w_ref, o_ref):
    o_ref[...] = lax.dot_general(a_ref[...], w_ref[...], _NT,
                                 preferred_element_type=jnp.float32).astype(o_ref.dtype)


def matmul_nt(a, wt, out_dtype, segments, tm=1024, tn=1024, name="mm_nt"):
    M, K = a.shape
    tm = _tile(M, tm)
    nrows = sum(n for _, n in segments)
    tn = _tile(math.gcd(*[n for _, n in segments], tn), tn)
    assert all(r0 % BF16_SUBLANES == 0 and n % tn == 0 for r0, n in segments)
    starts = [r0 + b * tn for r0, n in segments for b in range(n // tn)]

    def w_row(j):
        off = jnp.int32(starts[0] // BF16_SUBLANES)
        for b, s in enumerate(starts[1:], 1):
            off = jnp.where(j >= b, jnp.int32(s // BF16_SUBLANES), off)
        return off * BF16_SUBLANES

    return pl.pallas_call(
        _mm_nt_kernel, grid=(nrows // tn, M // tm),
        in_specs=[pl.BlockSpec((tm, K), lambda j, i: (i, 0)),
                  pl.BlockSpec((pl.Element(tn), pl.Element(K)), lambda j, i: (w_row(j), 0))],
        out_specs=pl.BlockSpec((tm, tn), lambda j, i: (i, j)),
        out_shape=jax.ShapeDtypeStruct((M, nrows), out_dtype),
        compiler_params=_params("parallel", "parallel"), name=name,
    )(a, wt)


def _conv_silu_kernel(seq, x_ref, prev_ref, next_ref, w_ref, s_ref, o_ref):
    tm = x_ref.shape[0]
    i = pl.program_id(0)
    x = x_ref[...]
    row = lax.broadcasted_iota(jnp.int32, x.shape, 0)
    first = (i * tm) % seq == 0
    last = ((i + 1) * tm) % seq == 0
    prev_row = jnp.where(first, 0.0, prev_ref[7:8, :])
    next_row = jnp.where(last, 0.0, next_ref[0:1, :])
    x_prev = jnp.where(row == 0, prev_row, pltpu.roll(x, 1, 0))
    x_next = jnp.where(row == tm - 1, next_row, pltpu.roll(x, tm - 1, 0))
    y = w_ref[0:1, :] * x_prev + w_ref[1:2, :] * x + w_ref[2:3, :] * x_next
    y = y * _sigmoid(y) * s_ref[...]
    o_ref[...] = y.astype(o_ref.dtype)


def conv_silu(qk, conv_w, col_scale, seq):
    T, C = qk.shape
    tm = _tile(seq, 512)
    tc = _tile(C, 1024)
    nb8 = T // 8
    return pl.pallas_call(
        functools.partial(_conv_silu_kernel, seq), grid=(T // tm, C // tc),
        in_specs=[pl.BlockSpec((tm, tc), lambda i, j: (i, j)),
                  pl.BlockSpec((8, tc), lambda i, j: (jnp.maximum(i * (tm // 8) - 1, 0), j)),
                  pl.BlockSpec((8, tc), lambda i, j: (jnp.minimum((i + 1) * (tm // 8), nb8 - 1), j)),
                  pl.BlockSpec((3, tc), lambda i, j: (0, j)),
                  pl.BlockSpec((1, tc), lambda i, j: (0, j))],
        out_specs=pl.BlockSpec((tm, tc), lambda i, j: (i, j)),
        out_shape=jax.ShapeDtypeStruct((T, C), jnp.bfloat16),
        compiler_params=_params("parallel", "parallel"), name="conv_silu",
    )(qk, qk, qk, conv_w, col_scale)


def _log_sigmoid(x):
    return jnp.minimum(x, 0.0) - jnp.log(1.0 + jnp.exp(-jnp.abs(x)))


LOG2E = math.log2(math.e)


def _mlstm_chunk(q, k, v, gi_row, gf_row, c_ref, n_ref, m_ref, reverse):
    L = q.shape[0]
    t_idx = lax.broadcasted_iota(jnp.int32, (L, L), 0)
    s_idx = lax.broadcasted_iota(jnp.int32, (L, L), 1)
    seen = (s_idx >= t_idx) if reverse else (s_idx <= t_idx)
    eye = s_idx == t_idx
    f_row = _log_sigmoid(gf_row) * LOG2E
    gi_row = gi_row * LOG2E
    b_col = jnp.sum(jnp.where(seen, f_row, 0.0), axis=1, keepdims=True)
    b_row = jnp.sum(jnp.where(eye, b_col, 0.0), axis=0, keepdims=True)
    i_col = jnp.sum(jnp.where(eye, gi_row, 0.0), axis=1, keepdims=True)
    b_last = jnp.sum(f_row, axis=1, keepdims=True)
    m_prev = m_ref[...]

    dmat = jnp.where(seen, b_col - b_row + gi_row, -jnp.inf)
    inter = b_col + m_prev
    m_t = jnp.maximum(inter, jnp.max(dmat, axis=1, keepdims=True))
    w_inter = jnp.exp2(inter - m_t)
    s_qk = lax.dot_general(q, k, _NT, preferred_element_type=jnp.float32) * jnp.exp2(dmat - m_t)
    c_prev = c_ref[...]
    num = w_inter * jnp.dot(q, c_prev.astype(jnp.bfloat16), preferred_element_type=jnp.float32)
    num = num + jnp.dot(s_qk.astype(jnp.bfloat16), v, preferred_element_type=jnp.float32)
    qn = jnp.sum(q.astype(jnp.float32) * n_ref[...], axis=1, keepdims=True)
    den = w_inter * qn + jnp.sum(s_qk, axis=1, keepdims=True)
    h = num / jnp.maximum(jnp.abs(den), jnp.exp2(-m_t))

    dec_col = b_last - b_col + i_col
    m_new = jnp.maximum(b_last + m_prev, jnp.max(dec_col, axis=0, keepdims=True))
    a = jnp.exp2(b_last + m_prev - m_new)
    kw = k.astype(jnp.float32) * jnp.exp2(dec_col - m_new)
    c_ref[...] = a * c_prev + lax.dot_general(
        kw.astype(jnp.bfloat16), v, (((0,), (0,)), ((), ())), preferred_element_type=jnp.float32)
    n_ref[...] = a * n_ref[...] + jnp.sum(kw, axis=0, keepdims=True)
    m_ref[...] = m_new
    return h


def _mlstm_kernel(q_ref, k_ref, v_ref, g_ref, bias_ref, o_ref, hg_ref, out_ref, hf_ref, c_ref, n_ref, m_ref):
    L = MLSTM_CHUNK
    nc = q_ref.shape[0] // L

    def reset():
        c_ref[...] = jnp.zeros_like(c_ref)
        n_ref[...] = jnp.zeros_like(n_ref)
        m_ref[...] = jnp.zeros_like(m_ref)

    def chunk(c, reverse):
        r0 = pl.multiple_of(c * L, L)
        rows = pl.ds(r0, L)
        g = g_ref[0, 0, c] + bias_ref[0]
        gi, gf = (g[2:3], g[3:4]) if reverse else (g[0:1], g[1:2])
        h = _mlstm_chunk(q_ref[rows, :], k_ref[rows, :], v_ref[rows, :], gi, gf, c_ref, n_ref, m_ref, reverse)
        return rows, h

    reset()

    def fwd(c, carry):
        rows, h = chunk(c, False)
        hf_ref[rows, :] = h
        return carry

    lax.fori_loop(0, nc, fwd, 0)
    reset()

    def bwd(j, carry):
        rows, h = chunk(nc - 1 - j, True)
        h = h + hf_ref[rows, :]
        h = h * lax.rsqrt(jnp.mean(h * h, axis=-1, keepdims=True) + RMS_EPS) * hg_ref[...]
        out_ref[rows, :] = (_sigmoid(o_ref[rows, :].astype(jnp.float32)) * h).astype(out_ref.dtype)
        return carry

    lax.fori_loop(0, nc, bwd, 0)


def mlstm(qk_c, vog, gates, bias, head_norm_g, batch, seq):
    T = batch * seq
    L = MLSTM_CHUNK
    nc = seq // L
    return pl.pallas_call(
        _mlstm_kernel, grid=(batch, M_HEADS),
        in_specs=[pl.BlockSpec((seq, M_QK_DIM), lambda b, h: (b, h)),
                  pl.BlockSpec((seq, M_QK_DIM), lambda b, h: (b, M_HEADS + h)),
                  pl.BlockSpec((seq, M_V_DIM), lambda b, h: (b, h)),
                  pl.BlockSpec((1, 1, nc, 4, L), lambda b, h: (b, h, 0, 0, 0)),
                  pl.BlockSpec((1, 4, 1), lambda b, h: (h, 0, 0)),
                  pl.BlockSpec((seq, M_V_DIM), lambda b, h: (b, M_HEADS + h)),
                  pl.BlockSpec((1, M_V_DIM), lambda b, h: (0, h))],
        out_specs=pl.BlockSpec((seq, M_V_DIM), lambda b, h: (b, h)),
        out_shape=jax.ShapeDtypeStruct((T, M_V), jnp.bfloat16),
        scratch_shapes=[pltpu.VMEM((seq, M_V_DIM), jnp.float32),
                        pltpu.VMEM((M_QK_DIM, M_V_DIM), jnp.float32),
                        pltpu.VMEM((1, M_QK_DIM), jnp.float32),
                        pltpu.VMEM((1, 1), jnp.float32)],
        compiler_params=_params("parallel", "parallel"), name="mlstm",
    )(qk_c, qk_c, vog, gates, bias, vog, head_norm_g.reshape(1, M_V))


def _rms(x, g_ref):
    return x * lax.rsqrt(jnp.mean(x * x, axis=-1, keepdims=True) + RMS_EPS) * g_ref[...]


def _rope_tile(t, cs):
    y = t * cs
    y = y + pltpu.roll(y, A_ROPE, 1)
    lane = lax.broadcasted_iota(jnp.int32, y.shape, 1)
    return jnp.where(lane < A_ROPE, y, 0.0)


def _mla_prep_kernel(cq_ref, ckv_ref, kr_ref, cs_ref, qg_ref, kvg_ref, cqn_ref, ckvn_ref, krope_ref):
    cqn_ref[...] = _rms(cq_ref[...], qg_ref).astype(jnp.bfloat16)
    ckvn_ref[...] = _rms(ckv_ref[...], kvg_ref).astype(jnp.bfloat16)
    krope_ref[...] = _rope_tile(kr_ref[...], cs_ref[...]).astype(jnp.bfloat16)


def mla_prep(small, cs, q_norm_g, kv_norm_g, seq):
    T = small.shape[0]
    tm = _tile(seq, 512)
    ns = seq // tm
    return pl.pallas_call(
        _mla_prep_kernel, grid=(T // tm,),
        in_specs=[pl.BlockSpec((tm, Q_LORA), lambda i: (i, 0)),
                  pl.BlockSpec((tm, KV_LORA), lambda i: (i, SMALL_CKV // KV_LORA)),
                  pl.BlockSpec((tm, V7X_LANES), lambda i: (i, SMALL_KR // V7X_LANES)),
                  pl.BlockSpec((tm, V7X_LANES), lambda i: (i % ns, 0)),
                  pl.BlockSpec((1, Q_LORA), lambda i: (0, 0)),
                  pl.BlockSpec((1, KV_LORA), lambda i: (0, 0))],
        out_specs=[pl.BlockSpec((tm, Q_LORA), lambda i: (i, 0)),
                   pl.BlockSpec((tm, KV_LORA), lambda i: (i, 0)),
                   pl.BlockSpec((tm, V7X_LANES), lambda i: (i, 0))],
        out_shape=[jax.ShapeDtypeStruct((T, Q_LORA), jnp.bfloat16),
                   jax.ShapeDtypeStruct((T, KV_LORA), jnp.bfloat16),
                   jax.ShapeDtypeStruct((T, V7X_LANES), jnp.bfloat16)],
        compiler_params=_params("parallel"), name="mla_prep",
    )(small, small, small, cs, q_norm_g.reshape(1, Q_LORA), kv_norm_g.reshape(1, KV_LORA))


Q_HEAD_W = 2 * V7X_LANES
Q_SCALE = ATTN_SCALE * math.log2(math.e)


def _q_up_kernel(a_ref, w_ref, cs_ref, o_ref):
    acc = jnp.dot(a_ref[...], w_ref[...], preferred_element_type=jnp.float32)
    cs = cs_ref[...]
    for hh in range(acc.shape[1] // Q_HEAD_W):
        c0 = hh * Q_HEAD_W
        o_ref[:, c0:c0 + A_NOPE] = (acc[:, c0:c0 + A_NOPE] * Q_SCALE).astype(o_ref.dtype)
        r = _rope_tile(acc[:, c0 + A_NOPE:c0 + Q_HEAD_W], cs)
        o_ref[:, c0 + A_NOPE:c0 + Q_HEAD_W] = (r * Q_SCALE).astype(o_ref.dtype)


def q_up(cqn, w_uq_r, cs, seq):
    T, K = cqn.shape
    N = w_uq_r.shape[1]
    tm = _tile(seq, 1024)
    tn = 1024
    ns = seq // tm
    return pl.pallas_call(
        _q_up_kernel, grid=(N // tn, T // tm),
        in_specs=[pl.BlockSpec((tm, K), lambda j, i: (i, 0)),
                  pl.BlockSpec((K, tn), lambda j, i: (0, j)),
                  pl.BlockSpec((tm, V7X_LANES), lambda j, i: (i % ns, 0))],
        out_specs=pl.BlockSpec((tm, tn), lambda j, i: (i, j)),
        out_shape=jax.ShapeDtypeStruct((T, N), jnp.bfloat16),
        compiler_params=_params("parallel", "parallel"), name="q_up",
    )(cqn, w_uq_r, cs)


ATTN_Q_TILE = 4096
ATTN_KV_CHUNK = 512


def _attn_kernel(q_ref, kn_ref, kr_ref, v_ref, o_ref, kcat_ref, vcat_ref):
    @pl.when(pl.program_id(2) == 0)
    def _():
        kcat_ref[:, :A_NOPE] = kn_ref[...]
        kcat_ref[:, A_NOPE:] = kr_ref[...]
        vcat_ref[:, :A_V] = v_ref[...]
        vcat_ref[:, A_V:] = jnp.ones((vcat_ref.shape[0], vcat_ref.shape[1] - A_V), vcat_ref.dtype)

    q = q_ref[...]
    seq = kcat_ref.shape[0]
    ck = min(ATTN_KV_CHUNK, seq)
    m = jnp.full((q.shape[0], 1), -jnp.inf, jnp.float32)
    acc = jnp.zeros((q.shape[0], vcat_ref.shape[1]), jnp.float32)
    for c in range(seq // ck):
        rows = slice(c * ck, (c + 1) * ck)
        s = lax.dot_general(q, kcat_ref[rows, :], _NT, preferred_element_type=jnp.float32)
        m_new = jnp.maximum(m, jnp.max(s, axis=-1, keepdims=True))
        p = jnp.exp2(s - m_new).astype(jnp.bfloat16)
        acc = jnp.exp2(m - m_new) * acc + jnp.dot(p, vcat_ref[rows, :], preferred_element_type=jnp.float32)
        m = m_new
    o_ref[...] = (acc[:, :A_V] / acc[:, A_V:]).astype(o_ref.dtype)


def attention(q_cat, kv, krope, batch, seq):
    T = batch * seq
    tq = _tile(seq, ATTN_Q_TILE)
    nq = seq // tq
    return pl.pallas_call(
        _attn_kernel, grid=(batch, A_HEADS, nq),
        in_specs=[pl.BlockSpec((tq, Q_HEAD_W), lambda b, h, i: (b * nq + i, h)),
                  pl.BlockSpec((seq, A_NOPE), lambda b, h, i: (b, 2 * h)),
                  pl.BlockSpec((seq, V7X_LANES), lambda b, h, i: (b, 0)),
                  pl.BlockSpec((seq, A_V), lambda b, h, i: (b, 2 * h + 1))],
        out_specs=pl.BlockSpec((tq, A_V), lambda b, h, i: (b * nq + i, h)),
        out_shape=jax.ShapeDtypeStruct((T, A_HEADS * A_V), jnp.bfloat16),
        scratch_shapes=[pltpu.VMEM((seq, 2 * V7X_LANES), jnp.bfloat16),
                        pltpu.VMEM((seq, 2 * A_V), jnp.bfloat16)],
        compiler_params=_params("parallel", "parallel", "arbitrary"), name="attention",
    )(q_cat, kv, krope, kv)


def _merge_kernel(a_ref, b_ref, wa_ref, wb_ref, ga_ref, gb_ref, o_ref):
    ya = jnp.dot(a_ref[...], wa_ref[...], preferred_element_type=jnp.float32)
    yb = jnp.dot(b_ref[...], wb_ref[...], preferred_element_type=jnp.float32)
    y = _sigmoid(ga_ref[...].astype(jnp.float32)) * ya + _sigmoid(gb_ref[...].astype(jnp.float32)) * yb
    o_ref[...] = y.astype(o_ref.dtype)


def merge(hm, att, w_pm, w_pa, gates, tm=512, tn=1024):
    T, K = hm.shape
    N = w_pm.shape[1]
    tm = _tile(T, tm)
    nb = N // tn
    a_spec = pl.BlockSpec((tm, K), lambda j, i: (i, 0))
    w_spec = pl.BlockSpec((K, tn), lambda j, i: (0, j), pipeline_mode=pl.Buffered(1))
    return pl.pallas_call(
        _merge_kernel, grid=(nb, T // tm),
        in_specs=[a_spec, a_spec, w_spec, w_spec,
                  pl.BlockSpec((tm, tn), lambda j, i: (i, 2 * nb + j)),
                  pl.BlockSpec((tm, tn), lambda j, i: (i, 3 * nb + j))],
        out_specs=pl.BlockSpec((tm, tn), lambda j, i: (i, j)),
        out_shape=jax.ShapeDtypeStruct((T, N), jnp.bfloat16),
        compiler_params=_params("parallel", "parallel"), name="merge",
    )(hm, att, w_pm, w_pa, gates, gates)


ROUTER_N = N_GROUPS + N_EXPERTS


def _split2(x):
    hi = x.astype(jnp.bfloat16)
    mid = (x - hi.astype(jnp.float32)).astype(jnp.bfloat16)
    return hi, mid


_HI16 = 0xFFFF0000


def _pack_bf16_pair(lo, hi):
    lo_bits = pltpu.bitcast(lo.astype(jnp.float32), jnp.uint32)
    hi_bits = pltpu.bitcast(hi.astype(jnp.float32), jnp.uint32)
    return (lo_bits >> 16) | (hi_bits & jnp.uint32(_HI16))


def _pack_bf16_halves(xb):
    half = xb.shape[1] // 2
    return _pack_bf16_pair(xb[:, :half], xb[:, half:])


def _unpack_halves_f32(u):
    return pltpu.bitcast(u << 16, jnp.float32), pltpu.bitcast(u & jnp.uint32(_HI16), jnp.float32)


def _unpack_bf16_halves(u):
    lo, hi = _unpack_halves_f32(u)
    return lo.astype(jnp.bfloat16), hi.astype(jnp.bfloat16)


def _ln_router_kernel(x_ref, y_ref, g_ref, b_ref, w_ref, rb_ref, xf_ref, xp_ref, ids_ref, wts_ref):
    x1 = _ln_body(DEEPNORM_ALPHA * x_ref[...] + y_ref[...], g_ref, b_ref)
    xf_ref[...] = x1
    xh, xm = _split2(x1)
    xp_ref[...] = _pack_bf16_halves(xh)
    dot = functools.partial(jnp.dot, preferred_element_type=jnp.float32)
    a = dot(xh, w_ref[0])
    logits = (pltpu.roll(a, ROUTER_W - ROUTER_N, 1) + dot(xm, w_ref[1])) + a
    logits = logits + rb_ref[...]
    lane = lax.broadcasted_iota(jnp.int32, logits.shape, 1)
    neg = -jnp.inf
    gl = jnp.where(lane < N_GROUPS, logits, neg)
    ge = jnp.exp(gl - jnp.max(gl, axis=-1, keepdims=True))
    g_prob = ge / jnp.sum(ge, axis=-1, keepdims=True)
    g_top = jnp.max(g_prob, axis=-1, keepdims=True)
    g_idx = jnp.min(jnp.where(g_prob == g_top, lane, ROUTER_W), axis=-1, keepdims=True)
    e_lo = N_GROUPS + EXPERTS_PER_GROUP * g_idx
    e_in = jnp.where((lane >= e_lo) & (lane < e_lo + EXPERTS_PER_GROUP), logits, neg)
    v1 = jnp.max(e_in, axis=-1, keepdims=True)
    i1 = jnp.min(jnp.where(e_in == v1, lane, ROUTER_W), axis=-1, keepdims=True)
    e_in2 = jnp.where(lane == i1, neg, e_in)
    v2 = jnp.max(e_in2, axis=-1, keepdims=True)
    i2 = jnp.min(jnp.where(e_in2 == v2, lane, ROUTER_W), axis=-1, keepdims=True)
    e2 = jnp.exp(v2 - v1)
    w1 = 1.0 / (1.0 + e2) * g_top
    w2 = e2 / (1.0 + e2) * g_top
    ids_ref[...] = jnp.where(lane == 0, i1 - N_GROUPS, jnp.where(lane == 1, i2 - N_GROUPS, 0))
    wts_ref[...] = jnp.where(lane == 0, w1, jnp.where(lane == 1, w2, 0.0))


def ln_router(x, y, g, b, w_r2, b_r):
    T, D = x.shape
    tm = _tile(T, 256)
    row = pl.BlockSpec((tm, D), lambda i: (i, 0))
    vec = pl.BlockSpec((1, D), lambda i: (0, 0))
    small = pl.BlockSpec((tm, ROUTER_W), lambda i: (i, 0))
    return pl.pallas_call(
        _ln_router_kernel, grid=(T // tm,),
        in_specs=[row, row, vec, vec,
                  pl.BlockSpec((2, D, ROUTER_W), lambda i: (0, 0, 0)),
                  pl.BlockSpec((1, ROUTER_W), lambda i: (0, 0))],
        out_specs=[row, pl.BlockSpec((tm, D // 2), lambda i: (i, 0)), small, small],
        out_shape=[jax.ShapeDtypeStruct((T, D), jnp.float32),
                   jax.ShapeDtypeStruct((T, D // 2), jnp.uint32),
                   jax.ShapeDtypeStruct((T, ROUTER_W), jnp.int32),
                   jax.ShapeDtypeStruct((T, ROUTER_W), jnp.float32)],
        compiler_params=_params("parallel"), name="ln_router",
    )(x, y, g.reshape(1, D), b.reshape(1, D), w_r2, b_r)


GATHER_UNROLL = 8
DMA_PRIORITIES = 2


def _row_copy(src_ref, src_row, dst_ref, r, sem):
    return pltpu.make_async_copy(src_ref.at[pl.ds(src_row, 1), :], dst_ref.at[pl.ds(r, 1), :], sem)


def _start_row_gather(idx_ref, base, src_ref, dst_ref, sem):
    def start(g, carry):
        for k in range(GATHER_UNROLL):
            r = g * GATHER_UNROLL + k
            _row_copy(src_ref, idx_ref[base + r], dst_ref, r, sem).start(priority=k % DMA_PRIORITIES)
        return carry

    lax.fori_loop(0, dst_ref.shape[0] // GATHER_UNROLL, start, 0)


def _wait_row_gather(src_ref, dst_ref, sem):
    def wait(r, carry):
        _row_copy(src_ref, 0, dst_ref, r, sem).wait()
        return carry

    lax.fori_loop(0, dst_ref.shape[0], wait, 0, unroll=GATHER_UNROLL)


def _gather_unpack_kernel(idx_ref, tv_ref, src_ref, o_ref, buf_ref, sem):
    i = pl.program_id(0)

    @pl.when(tv_ref[i] == 1)
    def _():
        _start_row_gather(idx_ref, i * o_ref.shape[0], src_ref, buf_ref, sem)
        _wait_row_gather(src_ref, buf_ref, sem)
        lo, hi = _unpack_bf16_halves(buf_ref[...])
        half = lo.shape[1]
        o_ref[:, :half] = lo
        o_ref[:, half:] = hi

    @pl.when(tv_ref[i] == 0)
    def _():
        o_ref[...] = jnp.zeros_like(o_ref)


def gather_expert_rows(src, idx, tile_v):
    R = idx.shape[0]
    D = src.shape[1]
    rows = MOE_TILE
    assert R % rows == 0 and rows % GATHER_UNROLL == 0
    return pl.pallas_call(
        _gather_unpack_kernel,
        grid_spec=pltpu.PrefetchScalarGridSpec(
            num_scalar_prefetch=2, grid=(R // rows,),
            in_specs=[pl.BlockSpec(memory_space=pl.ANY)],
            out_specs=pl.BlockSpec((rows, 2 * D), lambda i, idx, tv: (i, 0)),
            scratch_shapes=[pltpu.VMEM((rows, D), src.dtype), pltpu.SemaphoreType.DMA(())]),
        out_shape=jax.ShapeDtypeStruct((R, 2 * D), jnp.bfloat16),
        compiler_params=_params("arbitrary"), name="gather_rows",
    )(idx, tile_v, src)


def _expert_changed(te_ref, i):
    return (i == 0) | (te_ref[i] != te_ref[jnp.maximum(i - 1, 0)])


def _moe_up_kernel(te_ref, tv_ref, x_ref, wg_ref, wu_ref, h_ref, wgb_ref, wub_ref):
    i = pl.program_id(1)

    @pl.when(_expert_changed(te_ref, i))
    def _():
        wgb_ref[...] = wg_ref[0].astype(jnp.bfloat16)
        wub_ref[...] = wu_ref[0].astype(jnp.bfloat16)

    @pl.when(tv_ref[i] == 1)
    def _():
        x = x_ref[...]
        g = jnp.dot(x, wgb_ref[...], preferred_element_type=jnp.float32)
        u = jnp.dot(x, wub_ref[...], preferred_element_type=jnp.float32)
        h_ref[...] = (g * _sigmoid(g) * u).astype(h_ref.dtype)

    @pl.when(tv_ref[i] == 0)
    def _():
        h_ref[...] = jnp.zeros_like(h_ref)


def moe_up(xs, w_gate, w_up, tile_e, tile_v, tn=512):
    P = xs.shape[0]
    tm = MOE_TILE
    _, D, F = w_gate.shape
    w_spec = pl.BlockSpec((1, D, tn), lambda j, i, te, tv: (te[i], 0, j))
    return pl.pallas_call(
        _moe_up_kernel,
        grid_spec=pltpu.PrefetchScalarGridSpec(
            num_scalar_prefetch=2, grid=(F // tn, P // tm),
            in_specs=[pl.BlockSpec((tm, D), lambda j, i, te, tv: (i, 0)), w_spec, w_spec],
            out_specs=pl.BlockSpec((tm, tn), lambda j, i, te, tv: (i, j)),
            scratch_shapes=[pltpu.VMEM((D, tn), jnp.bfloat16), pltpu.VMEM((D, tn), jnp.bfloat16)]),
        out_shape=jax.ShapeDtypeStruct((P, F), jnp.bfloat16),
        compiler_params=_params("arbitrary", "arbitrary"), name="moe_up",
    )(tile_e, tile_v, xs, w_gate, w_up)


def _moe_down_kernel(te_ref, tv_ref, h_ref, wa_ref, wb_ref, y_ref, wab_ref, wbb_ref):
    i = pl.program_id(1)

    @pl.when(_expert_changed(te_ref, i))
    def _():
        wab_ref[...] = wa_ref[0].astype(jnp.bfloat16)
        wbb_ref[...] = wb_ref[0].astype(jnp.bfloat16)

    @pl.when(tv_ref[i] == 1)
    def _():
        h = h_ref[...]
        ya = jnp.dot(h, wab_ref[...], preferred_element_type=jnp.float32)
        yb = jnp.dot(h, wbb_ref[...], preferred_element_type=jnp.float32)
        y_ref[...] = _pack_bf16_pair(ya.astype(jnp.bfloat16), yb.astype(jnp.bfloat16))

    @pl.when(tv_ref[i] == 0)
    def _():
        y_ref[...] = jnp.zeros_like(y_ref)


def moe_down(h, w_down, tile_e, tile_v, tn=1024):
    P, F = h.shape
    tm = MOE_TILE
    half = w_down.shape[2] // 2
    nj = half // tn
    return pl.pallas_call(
        _moe_down_kernel,
        grid_spec=pltpu.PrefetchScalarGridSpec(
            num_scalar_prefetch=2, grid=(nj, P // tm),
            in_specs=[pl.BlockSpec((tm, F), lambda j, i, te, tv: (i, 0)),
                      pl.BlockSpec((1, F, tn), lambda j, i, te, tv: (te[i], 0, j)),
                      pl.BlockSpec((1, F, tn), lambda j, i, te, tv: (te[i], 0, j + nj))],
            out_specs=pl.BlockSpec((tm, tn), lambda j, i, te, tv: (i, j)),
            scratch_shapes=[pltpu.VMEM((F, tn), jnp.bfloat16), pltpu.VMEM((F, tn), jnp.bfloat16)]),
        out_shape=jax.ShapeDtypeStruct((P, half), jnp.uint32),
        compiler_params=_params("arbitrary", "arbitrary"), name="moe_down",
    )(tile_e, tile_v, h, w_down, w_down)


def _ln_out_kernel(pos_ref, x_ref, w_ref, g_ref, b_ref, y_ref, o_ref, buf_ref, sem):
    i = pl.program_id(0)
    n = pl.num_programs(0)
    tm = x_ref.shape[0]

    def parts(step, slot):
        return [(k * n * tm + step * tm, buf_ref.at[slot, pl.ds(k * tm, tm)]) for k in range(TOP_K)]

    def start(step, slot):
        for base, dst in parts(step, slot):
            _start_row_gather(pos_ref, base, y_ref, dst, sem.at[slot])

    @pl.when(i == 0)
    def _():
        start(0, 0)

    @pl.when(i + 1 < n)
    def _():
        start(i + 1, (i + 1) % 2)

    slot = i % 2
    for _, dst in parts(0, slot):
        _wait_row_gather(y_ref, dst, sem.at[slot])

    yb = buf_ref[slot]
    lo0, hi0 = _unpack_halves_f32(yb[:tm])
    lo1, hi1 = _unpack_halves_f32(yb[tm:])
    w = w_ref[...]
    w0, w1 = w[:, 0:1], w[:, 1:2]
    x = x_ref[...]
    half = lo0.shape[1]
    z = jnp.concatenate([DEEPNORM_ALPHA * x[:, :half] + (w0 * lo0 + w1 * lo1),
                         DEEPNORM_ALPHA * x[:, half:] + (w0 * hi0 + w1 * hi1)], axis=1)
    o_ref[...] = _ln_body(z, g_ref, b_ref)


def ln_out(x, y, pos, wts, g, b):
    T, D = x.shape
    tm = _tile(T, 256)
    row = pl.BlockSpec((tm, D), lambda i, pos: (i, 0))
    vec = pl.BlockSpec((1, D), lambda i, pos: (0, 0))
    return pl.pallas_call(
        _ln_out_kernel,
        grid_spec=pltpu.PrefetchScalarGridSpec(
            num_scalar_prefetch=1, grid=(T // tm,),
            in_specs=[row, pl.BlockSpec((tm, ROUTER_W), lambda i, pos: (i, 0)), vec, vec,
                      pl.BlockSpec(memory_space=pl.ANY)],
            out_specs=row,
            scratch_shapes=[pltpu.VMEM((2, TOP_K * tm, D // 2), jnp.uint32), pltpu.SemaphoreType.DMA((2,))]),
        out_shape=jax.ShapeDtypeStruct((T, D), jnp.float32),
        compiler_params=_params("arbitrary"), name="ln_out",
    )(pos, x, wts, g.reshape(1, D), b.reshape(1, D), y)


def _rot_cols(w):
    half = w.shape[-1] // 2
    return jnp.concatenate([-w[..., half:], w[..., :half]], axis=-1)


IN_OFFS = tuple(int(o) for o in np.cumsum((0,) + IN_SIZES))


def _prep_w_in_t(w_in):
    wt = w_in.T.astype(jnp.bfloat16)
    g, cq, ckv, kr = (wt[IN_OFFS[n]:IN_OFFS[n + 1]] for n in range(4, 8))
    kr_rot = _rot_cols(kr.T).T
    pad = jnp.zeros((SMALL_W - SMALL_GATES - g.shape[0], wt.shape[1]), wt.dtype)
    return wt, jnp.concatenate([cq, ckv, kr, kr_rot, g, pad], axis=0)


def _prep_w_uq(w_uq):
    w = w_uq.reshape(Q_LORA, A_HEADS, A_NOPE + A_ROPE)
    rope = w[:, :, A_NOPE:]
    w = jnp.concatenate([w[:, :, :A_NOPE], rope, _rot_cols(rope)], axis=-1)
    return w.reshape(Q_LORA, A_HEADS * Q_HEAD_W).astype(jnp.bfloat16)


def _rope_table(seq):
    inv = 1.0 / (ROPE_THETA ** (jnp.arange(0, A_ROPE, 2, dtype=jnp.float32) / A_ROPE))
    ang = jnp.arange(seq, dtype=jnp.float32)[:, None] * inv[None, :]
    cos, sin = jnp.cos(ang), jnp.sin(ang)
    return jnp.concatenate([cos, cos, sin, sin], axis=-1)


def _route_metadata(ids, T):
    tm = MOE_TILE
    n_tiles = (TOP_K * T) // tm + N_EXPERTS
    P = n_tiles * tm
    eid = jnp.concatenate([ids[:, 0], ids[:, 1]])
    tok = jnp.concatenate([jnp.arange(T, dtype=jnp.int32)] * TOP_K)
    onehot = (eid[:, None] == jnp.arange(N_EXPERTS, dtype=jnp.int32)[None, :]).astype(jnp.int32)
    rank = jnp.sum((jnp.cumsum(onehot, axis=0) - onehot) * onehot, axis=1)
    counts = jnp.sum(onehot, axis=0)
    padded = ((counts + tm - 1) // tm) * tm
    ends = jnp.cumsum(padded)
    starts = ends - padded
    pos = (starts[eid] + rank).astype(jnp.int32)
    src = (jnp.arange(P, dtype=jnp.int32) % T).at[pos].set(tok)
    tile_start = jnp.arange(n_tiles, dtype=jnp.int32) * tm
    tile_e = jnp.minimum(jnp.searchsorted(ends, tile_start, side="right"), N_EXPERTS - 1).astype(jnp.int32)
    tile_v = (tile_start < ends[-1]).astype(jnp.int32)
    return src, pos, tile_e, tile_v


def kernel(x, ln0_g, ln0_b, w_in, b_gates, conv_qk, head_norm_g, q_norm_g, w_uq, kv_norm_g, w_ukv, w_pm, w_pa,
           w_out, ln1_g, ln1_b, w_rg, b_rg, w_re, b_re, w_e_gate, w_e_up, w_e_down, ln2_g, ln2_b):
    B, S, D = x.shape
    T = B * S
    bf16 = jnp.bfloat16
    cs = _rope_table(S)
    xf, xb = layer_norm_in(x.reshape(T, D), ln0_g, ln0_b)
    for l in range(DEPTH):
        w_in_t, w_small_t = _prep_w_in_t(w_in[l])
        qk = matmul_nt(xb, w_in_t, jnp.float32, [(IN_OFFS[0], 2 * M_QK)], name="proj_qk")
        v_o_g = matmul_nt(xb, w_in_t, bf16, [(IN_OFFS[2], 2 * M_V), (IN_OFFS[8], 2 * D_MODEL)], name="proj_vog")
        small = matmul_nt(xb, w_small_t, jnp.float32, [(0, SMALL_W)], name="proj_small")

        k_scale = jnp.concatenate([jnp.ones((1, M_QK), jnp.float32),
                                   jnp.full((1, M_QK), M_QK_DIM ** -0.5, jnp.float32)], axis=1)
        qk_c = conv_silu(qk, conv_qk[l], k_scale, S)
        nc = S // MLSTM_CHUNK
        gates = small[:, SMALL_GATES:SMALL_GATES + 4 * M_HEADS].reshape(B, nc, MLSTM_CHUNK, 4, M_HEADS)
        gates = gates.transpose(0, 4, 1, 3, 2)
        bias = b_gates[l].reshape(4, M_HEADS).T.reshape(M_HEADS, 4, 1)
        hm = mlstm(qk_c, v_o_g, gates, bias, head_norm_g[l], B, S)

        cqn, ckvn, krope = mla_prep(small, cs, q_norm_g[l], kv_norm_g[l], S)
        q_cat = q_up(cqn, _prep_w_uq(w_uq[l]), cs, S)
        kv = matmul(ckvn, w_ukv[l].astype(bf16), bf16, name="kv_up")
        att = attention(q_cat, kv, krope, B, S)

        merged = merge(hm, att, w_pm[l].astype(bf16), w_pa[l].astype(bf16), v_o_g)
        mixed = matmul(merged, w_out[l].astype(bf16), jnp.float32, name="proj_out")

        w_hi, w_mid = _split2(jnp.concatenate([w_rg[l], w_re[l]], axis=1))
        w_r2 = jnp.stack([jnp.pad(jnp.concatenate([w_hi, w_mid], axis=1), ((0, 0), (0, ROUTER_W - 2 * ROUTER_N))),
                          jnp.pad(w_hi, ((0, 0), (0, ROUTER_W - ROUTER_N)))])
        b_r = jnp.pad(jnp.concatenate([b_rg[l], b_re[l]]), (0, ROUTER_W - ROUTER_N))
        x1, x1p, ids, wts = ln_router(xf, mixed, ln1_g[l], ln1_b[l], w_r2, b_r.reshape(1, ROUTER_W))

        src, pos, tile_e, tile_v = _route_metadata(ids, T)
        xs = gather_expert_rows(x1p, src, tile_v)
        h = moe_up(xs, w_e_gate[l], w_e_up[l], tile_e, tile_v)
        y = moe_down(h, w_e_down[l], tile_e, tile_v)
        xf = ln_out(x1, y, pos, wts, ln2_g[l], ln2_b[l])
        if l + 1 < DEPTH:
            xb = xf.astype(bf16)
    return xf.reshape(B, S, D)
```

```python
import functools
import math

import jax
import jax.numpy as jnp
import numpy as np
from jax import lax
from jax.experimental import pallas as pl
from jax.experimental.pallas import tpu as pltpu

D_MODEL = 4096
M_HEADS = 8
M_QK_DIM = 256
M_V_DIM = 512
M_QK = M_HEADS * M_QK_DIM
M_V = M_HEADS * M_V_DIM
A_HEADS = 32
A_NOPE = 128
A_ROPE = 64
A_V = 128
Q_LORA = 1024
KV_LORA = 512
ROPE_THETA = 10000.0
ATTN_SCALE = (A_NOPE + A_ROPE) ** -0.5
N_GROUPS = 8
EXPERTS_PER_GROUP = 4
N_EXPERTS = 32
TOP_K = 2
D_EXPERT = 1024
LN_EPS = 1e-5
RMS_EPS = 1e-6
DEPTH = 1
DEEPNORM_ALPHA = (2 * DEPTH) ** 0.25
IN_SIZES = (M_QK, M_QK, M_V, M_V, 4 * M_HEADS, Q_LORA, KV_LORA, A_ROPE, D_MODEL, D_MODEL)

V7X_LANES = 128
BF16_SUBLANES = 16
V7X_VMEM_LIMIT = 56 * 1024 * 1024

MLSTM_CHUNK = 256
MOE_TILE = 512
SMALL_W = 2048
SMALL_CKV = Q_LORA
SMALL_KR = Q_LORA + KV_LORA
SMALL_GATES = SMALL_KR + 2 * A_ROPE
ROUTER_W = 128


def _params(*sem):
    return pltpu.CompilerParams(dimension_semantics=sem, vmem_limit_bytes=V7X_VMEM_LIMIT)


def _tile(n, pref):
    t = min(n, pref)
    while n % t:
        t //= 2
    return t


def _sigmoid(x):
    return 1.0 / (1.0 + jnp.exp(-x))


def _ln_body(z, g_ref, b_ref):
    mu = jnp.mean(z, axis=-1, keepdims=True)
    zc = z - mu
    var = jnp.mean(zc * zc, axis=-1, keepdims=True)
    return zc * lax.rsqrt(var + LN_EPS) * g_ref[...] + b_ref[...]


def _ln_kernel(x_ref, g_ref, b_ref, of_ref, ob_ref):
    y = _ln_body(x_ref[...], g_ref, b_ref)
    of_ref[...] = y
    ob_ref[...] = y.astype(jnp.bfloat16)


def layer_norm_in(x, g, b):
    T, D = x.shape
    tm = _tile(T, 256)
    row = pl.BlockSpec((tm, D), lambda i: (i, 0))
    vec = pl.BlockSpec((1, D), lambda i: (0, 0))
    return pl.pallas_call(
        _ln_kernel, grid=(T // tm,), in_specs=[row, vec, vec], out_specs=[row, row],
        out_shape=[jax.ShapeDtypeStruct((T, D), jnp.float32), jax.ShapeDtypeStruct((T, D), jnp.bfloat16)],
        compiler_params=_params("parallel"), name="ln_in",
    )(x, g.reshape(1, D), b.reshape(1, D))


def _mm_kernel(a_ref, w_ref, o_ref):
    o_ref[...] = jnp.dot(a_ref[...], w_ref[...], preferred_element_type=jnp.float32).astype(o_ref.dtype)


def matmul(a, w, out_dtype, col0=0, ncols=None, tm=1024, tn=1024, name="mm"):
    M, K = a.shape
    ncols = w.shape[1] - col0 if ncols is None else ncols
    tm = _tile(M, tm)
    tn = _tile(ncols, tn)
    assert col0 % tn == 0
    cb = col0 // tn
    return pl.pallas_call(
        _mm_kernel, grid=(ncols // tn, M // tm),
        in_specs=[pl.BlockSpec((tm, K), lambda j, i: (i, 0)),
                  pl.BlockSpec((K, tn), lambda j, i: (0, j + cb))],
        out_specs=pl.BlockSpec((tm, tn), lambda j, i: (i, j)),
        out_shape=jax.ShapeDtypeStruct((M, ncols), out_dtype),
        compiler_params=_params("parallel", "parallel"), name=name,
    )(a, w)


_NT = (((1,), (1,)), ((), ()))


def _mm_nt_kernel(a_ref, w_ref, o_ref):
    o_ref[...] = lax.dot_general(a_ref[...], w_ref[...], _NT,
                                 preferred_element_type=jnp.float32).astype(o_ref.dtype)


def matmul_nt(a, wt, out_dtype, segments, tm=1024, tn=1024, name="mm_nt"):
    M, K = a.shape
    tm = _tile(M, tm)
    nrows = sum(n for _, n in segments)
    tn = _tile(math.gcd(*[n for _, n in segments], tn), tn)
    assert all(r0 % BF16_SUBLANES == 0 and n % tn == 0 for r0, n in segments)
    starts = [r0 + b * tn for r0, n in segments for b in range(n // tn)]

    def w_row(j):
        off = jnp.int32(starts[0] // BF16_SUBLANES)
        for b, s in enumerate(starts[1:], 1):
            off = jnp.where(j >= b, jnp.int32(s // BF16_SUBLANES), off)
        return off * BF16_SUBLANES

    return pl.pallas_call(
        _mm_nt_kernel, grid=(nrows // tn, M // tm),
        in_specs=[pl.BlockSpec((tm, K), lambda j, i: (i, 0)),
                  pl.BlockSpec((pl.Element(tn), pl.Element(K)), lambda j, i: (w_row(j), 0))],
        out_specs=pl.BlockSpec((tm, tn), lambda j, i: (i, j)),
        out_shape=jax.ShapeDtypeStruct((M, nrows), out_dtype),
        compiler_params=_params("parallel", "parallel"), name=name,
    )(a, wt)


def _conv_silu_kernel(seq, x_ref, prev_ref, next_ref, w_ref, s_ref, o_ref):
    tm = x_ref.shape[0]
    i = pl.program_id(0)
    x = x_ref[...].astype(jnp.float32)
    row = lax.broadcasted_iota(jnp.int32, x.shape, 0)
    first = (i * tm) % seq == 0
    last = ((i + 1) * tm) % seq == 0
    halo = prev_ref.shape[0]
    prev_row = jnp.where(first, 0.0, prev_ref[...].astype(jnp.float32)[halo - 1:halo, :])
    next_row = jnp.where(last, 0.0, next_ref[...].astype(jnp.float32)[0:1, :])
    x_prev = jnp.where(row == 0, prev_row, pltpu.roll(x, 1, 0))
    x_next = jnp.where(row == tm - 1, next_row, pltpu.roll(x, tm - 1, 0))
    y = w_ref[0:1, :] * x_prev + w_ref[1:2, :] * x + w_ref[2:3, :] * x_next
    y = y * _sigmoid(y) * s_ref[...]
    o_ref[...] = y.astype(o_ref.dtype)


def conv_silu(qk, conv_w, col_scale, seq):
    T, C = qk.shape
    tm = _tile(seq, 512)
    tc = _tile(C, 1024)
    halo = BF16_SUBLANES
    nbh = T // halo
    return pl.pallas_call(
        functools.partial(_conv_silu_kernel, seq), grid=(T // tm, C // tc),
        in_specs=[pl.BlockSpec((tm, tc), lambda i, j: (i, j)),
                  pl.BlockSpec((halo, tc), lambda i, j: (jnp.maximum(i * (tm // halo) - 1, 0), j)),
                  pl.BlockSpec((halo, tc), lambda i, j: (jnp.minimum((i + 1) * (tm // halo), nbh - 1), j)),
                  pl.BlockSpec((3, tc), lambda i, j: (0, j)),
                  pl.BlockSpec((1, tc), lambda i, j: (0, j))],
        out_specs=pl.BlockSpec((tm, tc), lambda i, j: (i, j)),
        out_shape=jax.ShapeDtypeStruct((T, C), jnp.bfloat16),
        compiler_params=_params("parallel", "parallel"), name="conv_silu",
    )(qk, qk, qk, conv_w, col_scale)


def _log_sigmoid(x):
    return jnp.minimum(x, 0.0) - jnp.log(1.0 + jnp.exp(-jnp.abs(x)))


LOG2E = math.log2(math.e)


def _mlstm_chunk(q, k, v, gi_row, gf_row, c_ref, n_ref, m_ref, reverse):
    L = q.shape[0]
    t_idx = lax.broadcasted_iota(jnp.int32, (L, L), 0)
    s_idx = lax.broadcasted_iota(jnp.int32, (L, L), 1)
    seen = (s_idx >= t_idx) if reverse else (s_idx <= t_idx)
    eye = s_idx == t_idx
    f_row = _log_sigmoid(gf_row) * LOG2E
    gi_row = gi_row * LOG2E
    b_col = jnp.sum(jnp.where(seen, f_row, 0.0), axis=1, keepdims=True)
    b_row = jnp.sum(jnp.where(eye, b_col, 0.0), axis=0, keepdims=True)
    i_col = jnp.sum(jnp.where(eye, gi_row, 0.0), axis=1, keepdims=True)
    b_last = jnp.sum(f_row, axis=1, keepdims=True)
    m_prev = m_ref[...]

    dmat = jnp.where(seen, b_col - b_row + gi_row, -jnp.inf)
    inter = b_col + m_prev
    m_t = jnp.maximum(inter, jnp.max(dmat, axis=1, keepdims=True))
    w_inter = jnp.exp2(inter - m_t)
    s_qk = lax.dot_general(q, k, _NT, preferred_element_type=jnp.float32) * jnp.exp2(dmat - m_t)
    c_prev = c_ref[...]
    num = w_inter * jnp.dot(q, c_prev.astype(jnp.bfloat16), preferred_element_type=jnp.float32)
    num = num + jnp.dot(s_qk.astype(jnp.bfloat16), v, preferred_element_type=jnp.float32)
    qn = jnp.sum(q.astype(jnp.float32) * n_ref[...], axis=1, keepdims=True)
    den = w_inter * qn + jnp.sum(s_qk, axis=1, keepdims=True)
    h = num / jnp.maximum(jnp.abs(den), jnp.exp2(-m_t))

    dec_col = b_last - b_col + i_col
    m_new = jnp.maximum(b_last + m_prev, jnp.max(dec_col, axis=0, keepdims=True))
    a = jnp.exp2(b_last + m_prev - m_new)
    kw = k.astype(jnp.float32) * jnp.exp2(dec_col - m_new)
    c_ref[...] = a * c_prev + lax.dot_general(
        kw.astype(jnp.bfloat16), v, (((0,), (0,)), ((), ())), preferred_element_type=jnp.float32)
    n_ref[...] = a * n_ref[...] + jnp.sum(kw, axis=0, keepdims=True)
    m_ref[...] = m_new
    return h


def _mlstm_kernel(q_ref, k_ref, v_ref, g_ref, bias_ref, o_ref, hg_ref, out_ref, hf_ref, c_ref, n_ref, m_ref):
    L = MLSTM_CHUNK
    nc = q_ref.shape[0] // L

    def reset():
        c_ref[...] = jnp.zeros_like(c_ref)
        n_ref[...] = jnp.zeros_like(n_ref)
        m_ref[...] = jnp.zeros_like(m_ref)

    def chunk(c, reverse):
        r0 = pl.multiple_of(c * L, L)
        rows = pl.ds(r0, L)
        g = g_ref[0, 0, c] + bias_ref[0]
        gi, gf = (g[2:3], g[3:4]) if reverse else (g[0:1], g[1:2])
        h = _mlstm_chunk(q_ref[rows, :], k_ref[rows, :], v_ref[rows, :], gi, gf, c_ref, n_ref, m_ref, reverse)
        return rows, h

    reset()

    def fwd(c, carry):
        rows, h = chunk(c, False)
        hf_ref[rows, :] = h
        return carry

    lax.fori_loop(0, nc, fwd, 0)
    reset()

    def bwd(j, carry):
        rows, h = chunk(nc - 1 - j, True)
        h = h + hf_ref[rows, :]
        h = h * lax.rsqrt(jnp.mean(h * h, axis=-1, keepdims=True) + RMS_EPS) * hg_ref[...]
        out_ref[rows, :] = (_sigmoid(o_ref[rows, :].astype(jnp.float32)) * h).astype(out_ref.dtype)
        return carry

    lax.fori_loop(0, nc, bwd, 0)


def mlstm(qk_c, vog, gates, bias, head_norm_g, batch, seq):
    T = batch * seq
    L = MLSTM_CHUNK
    nc = seq // L
    return pl.pallas_call(
        _mlstm_kernel, grid=(batch, M_HEADS),
        in_specs=[pl.BlockSpec((seq, M_QK_DIM), lambda b, h: (b, h)),
                  pl.BlockSpec((seq, M_QK_DIM), lambda b, h: (b, M_HEADS + h)),
                  pl.BlockSpec((seq, M_V_DIM), lambda b, h: (b, h)),
                  pl.BlockSpec((1, 1, nc, 4, L), lambda b, h: (b, h, 0, 0, 0)),
                  pl.BlockSpec((1, 4, 1), lambda b, h: (h, 0, 0)),
                  pl.BlockSpec((seq, M_V_DIM), lambda b, h: (b, M_HEADS + h)),
                  pl.BlockSpec((1, M_V_DIM), lambda b, h: (0, h))],
        out_specs=pl.BlockSpec((seq, M_V_DIM), lambda b, h: (b, h)),
        out_shape=jax.ShapeDtypeStruct((T, M_V), jnp.bfloat16),
        scratch_shapes=[pltpu.VMEM((seq, M_V_DIM), jnp.float32),
                        pltpu.VMEM((M_QK_DIM, M_V_DIM), jnp.float32),
                        pltpu.VMEM((1, M_QK_DIM), jnp.float32),
                        pltpu.VMEM((1, 1), jnp.float32)],
        compiler_params=_params("parallel", "parallel"), name="mlstm",
    )(qk_c, qk_c, vog, gates, bias, vog, head_norm_g.reshape(1, M_V))


def _rms(x, g_ref):
    return x * lax.rsqrt(jnp.mean(x * x, axis=-1, keepdims=True) + RMS_EPS) * g_ref[...]


def _rope_tile(t, cs):
    y = t * cs
    y = y + pltpu.roll(y, A_ROPE, 1)
    lane = lax.broadcasted_iota(jnp.int32, y.shape, 1)
    return jnp.where(lane < A_ROPE, y, 0.0)


def _mla_prep_kernel(cq_ref, ckv_ref, kr_ref, cs_ref, qg_ref, kvg_ref, cqn_ref, ckvn_ref, krope_ref):
    cqn_ref[...] = _rms(cq_ref[...], qg_ref).astype(jnp.bfloat16)
    ckvn_ref[...] = _rms(ckv_ref[...], kvg_ref).astype(jnp.bfloat16)
    krope_ref[...] = _rope_tile(kr_ref[...], cs_ref[...]).astype(jnp.bfloat16)


def mla_prep(small, cs, q_norm_g, kv_norm_g, seq):
    T = small.shape[0]
    tm = _tile(seq, 512)
    ns = seq // tm
    return pl.pallas_call(
        _mla_prep_kernel, grid=(T // tm,),
        in_specs=[pl.BlockSpec((tm, Q_LORA), lambda i: (i, 0)),
                  pl.BlockSpec((tm, KV_LORA), lambda i: (i, SMALL_CKV // KV_LORA)),
                  pl.BlockSpec((tm, V7X_LANES), lambda i: (i, SMALL_KR // V7X_LANES)),
                  pl.BlockSpec((tm, V7X_LANES), lambda i: (i % ns, 0)),
                  pl.BlockSpec((1, Q_LORA), lambda i: (0, 0)),
                  pl.BlockSpec((1, KV_LORA), lambda i: (0, 0))],
        out_specs=[pl.BlockSpec((tm, Q_LORA), lambda i: (i, 0)),
                   pl.BlockSpec((tm, KV_LORA), lambda i: (i, 0)),
                   pl.BlockSpec((tm, V7X_LANES), lambda i: (i, 0))],
        out_shape=[jax.ShapeDtypeStruct((T, Q_LORA), jnp.bfloat16),
                   jax.ShapeDtypeStruct((T, KV_LORA), jnp.bfloat16),
                   jax.ShapeDtypeStruct((T, V7X_LANES), jnp.bfloat16)],
        compiler_params=_params("parallel"), name="mla_prep",
    )(small, small, small, cs, q_norm_g.reshape(1, Q_LORA), kv_norm_g.reshape(1, KV_LORA))


Q_HEAD_W = 2 * V7X_LANES
Q_SCALE = ATTN_SCALE * math.log2(math.e)


def _q_up_kernel(a_ref, w_ref, cs_ref, o_ref):
    acc = jnp.dot(a_ref[...], w_ref[...], preferred_element_type=jnp.float32)
    cs = cs_ref[...]
    for hh in range(acc.shape[1] // Q_HEAD_W):
        c0 = hh * Q_HEAD_W
        o_ref[:, c0:c0 + A_NOPE] = (acc[:, c0:c0 + A_NOPE] * Q_SCALE).astype(o_ref.dtype)
        r = _rope_tile(acc[:, c0 + A_NOPE:c0 + Q_HEAD_W], cs)
        o_ref[:, c0 + A_NOPE:c0 + Q_HEAD_W] = (r * Q_SCALE).astype(o_ref.dtype)


def q_up(cqn, w_uq_r, cs, seq):
    T, K = cqn.shape
    N = w_uq_r.shape[1]
    tm = _tile(seq, 1024)
    tn = 1024
    ns = seq // tm
    return pl.pallas_call(
        _q_up_kernel, grid=(N // tn, T // tm),
        in_specs=[pl.BlockSpec((tm, K), lambda j, i: (i, 0)),
                  pl.BlockSpec((K, tn), lambda j, i: (0, j)),
                  pl.BlockSpec((tm, V7X_LANES), lambda j, i: (i % ns, 0))],
        out_specs=pl.BlockSpec((tm, tn), lambda j, i: (i, j)),
        out_shape=jax.ShapeDtypeStruct((T, N), jnp.bfloat16),
        compiler_params=_params("parallel", "parallel"), name="q_up",
    )(cqn, w_uq_r, cs)


ATTN_Q_TILE = 4096
ATTN_KV_CHUNK = 512


def _attn_kernel(q_ref, kn_ref, kr_ref, v_ref, o_ref, kcat_ref, vcat_ref):
    @pl.when(pl.program_id(2) == 0)
    def _():
        kcat_ref[:, :A_NOPE] = kn_ref[...]
        kcat_ref[:, A_NOPE:] = kr_ref[...]
        vcat_ref[:, :A_V] = v_ref[...]
        vcat_ref[:, A_V:] = jnp.ones((vcat_ref.shape[0], vcat_ref.shape[1] - A_V), vcat_ref.dtype)

    q = q_ref[...]
    seq = kcat_ref.shape[0]
    ck = min(ATTN_KV_CHUNK, seq)
    m = jnp.full((q.shape[0], 1), -jnp.inf, jnp.float32)
    acc = jnp.zeros((q.shape[0], vcat_ref.shape[1]), jnp.float32)
    for c in range(seq // ck):
        rows = slice(c * ck, (c + 1) * ck)
        s = lax.dot_general(q, kcat_ref[rows, :], _NT, preferred_element_type=jnp.float32)
        m_new = jnp.maximum(m, jnp.max(s, axis=-1, keepdims=True))
        p = jnp.exp2(s - m_new).astype(jnp.bfloat16)
        acc = jnp.exp2(m - m_new) * acc + jnp.dot(p, vcat_ref[rows, :], preferred_element_type=jnp.float32)
        m = m_new
    o_ref[...] = (acc[:, :A_V] / acc[:, A_V:]).astype(o_ref.dtype)


def attention(q_cat, kv, krope, batch, seq):
    T = batch * seq
    tq = _tile(seq, ATTN_Q_TILE)
    nq = seq // tq
    return pl.pallas_call(
        _attn_kernel, grid=(batch, A_HEADS, nq),
        in_specs=[pl.BlockSpec((tq, Q_HEAD_W), lambda b, h, i: (b * nq + i, h)),
                  pl.BlockSpec((seq, A_NOPE), lambda b, h, i: (b, 2 * h)),
                  pl.BlockSpec((seq, V7X_LANES), lambda b, h, i: (b, 0)),
                  pl.BlockSpec((seq, A_V), lambda b, h, i: (b, 2 * h + 1))],
        out_specs=pl.BlockSpec((tq, A_V), lambda b, h, i: (b * nq + i, h)),
        out_shape=jax.ShapeDtypeStruct((T, A_HEADS * A_V), jnp.bfloat16),
        scratch_shapes=[pltpu.VMEM((seq, 2 * V7X_LANES), jnp.bfloat16),
                        pltpu.VMEM((seq, 2 * A_V), jnp.bfloat16)],
        compiler_params=_params("parallel", "parallel", "arbitrary"), name="attention",
    )(q_cat, kv, krope, kv)


def _merge_kernel(a_ref, b_ref, wa_ref, wb_ref, ga_ref, gb_ref, o_ref):
    ya = jnp.dot(a_ref[...], wa_ref[...], preferred_element_type=jnp.float32)
    yb = jnp.dot(b_ref[...], wb_ref[...], preferred_element_type=jnp.float32)
    y = _sigmoid(ga_ref[...].astype(jnp.float32)) * ya + _sigmoid(gb_ref[...].astype(jnp.float32)) * yb
    o_ref[...] = y.astype(o_ref.dtype)


def merge(hm, att, w_pm, w_pa, gates, tm=512, tn=1024):
    T, K = hm.shape
    N = w_pm.shape[1]
    tm = _tile(T, tm)
    nb = N // tn
    a_spec = pl.BlockSpec((tm, K), lambda j, i: (i, 0))
    w_spec = pl.BlockSpec((K, tn), lambda j, i: (0, j), pipeline_mode=pl.Buffered(1))
    return pl.pallas_call(
        _merge_kernel, grid=(nb, T // tm),
        in_specs=[a_spec, a_spec, w_spec, w_spec,
                  pl.BlockSpec((tm, tn), lambda j, i: (i, 2 * nb + j)),
                  pl.BlockSpec((tm, tn), lambda j, i: (i, 3 * nb + j))],
        out_specs=pl.BlockSpec((tm, tn), lambda j, i: (i, j)),
        out_shape=jax.ShapeDtypeStruct((T, N), jnp.bfloat16),
        compiler_params=_params("parallel", "parallel"), name="merge",
    )(hm, att, w_pm, w_pa, gates, gates)


ROUTER_N = N_GROUPS + N_EXPERTS


def _split2(x):
    hi = x.astype(jnp.bfloat16)
    mid = (x - hi.astype(jnp.float32)).astype(jnp.bfloat16)
    return hi, mid


_HI16 = 0xFFFF0000


def _pack_bf16_pair(lo, hi):
    lo_bits = pltpu.bitcast(lo.astype(jnp.float32), jnp.uint32)
    hi_bits = pltpu.bitcast(hi.astype(jnp.float32), jnp.uint32)
    return (lo_bits >> 16) | (hi_bits & jnp.uint32(_HI16))


def _pack_bf16_halves(xb):
    half = xb.shape[1] // 2
    return _pack_bf16_pair(xb[:, :half], xb[:, half:])


def _unpack_halves_f32(u):
    return pltpu.bitcast(u << 16, jnp.float32), pltpu.bitcast(u & jnp.uint32(_HI16), jnp.float32)


def _unpack_bf16_halves(u):
    lo, hi = _unpack_halves_f32(u)
    return lo.astype(jnp.bfloat16), hi.astype(jnp.bfloat16)


def _ln_router_kernel(x_ref, y_ref, g_ref, b_ref, w_ref, rb_ref, xf_ref, xp_ref, ids_ref, wts_ref):
    x1 = _ln_body(DEEPNORM_ALPHA * x_ref[...] + y_ref[...], g_ref, b_ref)
    xf_ref[...] = x1
    xh, xm = _split2(x1)
    xp_ref[...] = _pack_bf16_halves(xh)
    dot = functools.partial(jnp.dot, preferred_element_type=jnp.float32)
    a = dot(xh, w_ref[0])
    logits = (pltpu.roll(a, ROUTER_W - ROUTER_N, 1) + dot(xm, w_ref[1])) + a
    logits = logits + rb_ref[...]
    lane = lax.broadcasted_iota(jnp.int32, logits.shape, 1)
    neg = -jnp.inf
    gl = jnp.where(lane < N_GROUPS, logits, neg)
    ge = jnp.exp(gl - jnp.max(gl, axis=-1, keepdims=True))
    g_prob = ge / jnp.sum(ge, axis=-1, keepdims=True)
    g_top = jnp.max(g_prob, axis=-1, keepdims=True)
    g_idx = jnp.min(jnp.where(g_prob == g_top, lane, ROUTER_W), axis=-1, keepdims=True)
    e_lo = N_GROUPS + EXPERTS_PER_GROUP * g_idx
    e_in = jnp.where((lane >= e_lo) & (lane < e_lo + EXPERTS_PER_GROUP), logits, neg)
    v1 = jnp.max(e_in, axis=-1, keepdims=True)
    i1 = jnp.min(jnp.where(e_in == v1, lane, ROUTER_W), axis=-1, keepdims=True)
    e_in2 = jnp.where(lane == i1, neg, e_in)
    v2 = jnp.max(e_in2, axis=-1, keepdims=True)
    i2 = jnp.min(jnp.where(e_in2 == v2, lane, ROUTER_W), axis=-1, keepdims=True)
    e2 = jnp.exp(v2 - v1)
    w1 = 1.0 / (1.0 + e2) * g_top
    w2 = e2 / (1.0 + e2) * g_top
    ids_ref[...] = jnp.where(lane == 0, i1 - N_GROUPS, jnp.where(lane == 1, i2 - N_GROUPS, 0))
    wts_ref[...] = jnp.where(lane == 0, w1, jnp.where(lane == 1, w2, 0.0))


def ln_router(x, y, g, b, w_r2, b_r):
    T, D = x.shape
    tm = _tile(T, 256)
    row = pl.BlockSpec((tm, D), lambda i: (i, 0))
    vec = pl.BlockSpec((1, D), lambda i: (0, 0))
    small = pl.BlockSpec((tm, ROUTER_W), lambda i: (i, 0))
    return pl.pallas_call(
        _ln_router_kernel, grid=(T // tm,),
        in_specs=[row, row, vec, vec,
                  pl.BlockSpec((2, D, ROUTER_W), lambda i: (0, 0, 0)),
                  pl.BlockSpec((1, ROUTER_W), lambda i: (0, 0))],
        out_specs=[row, pl.BlockSpec((tm, D // 2), lambda i: (i, 0)), small, small],
        out_shape=[jax.ShapeDtypeStruct((T, D), jnp.float32),
                   jax.ShapeDtypeStruct((T, D // 2), jnp.uint32),
                   jax.ShapeDtypeStruct((T, ROUTER_W), jnp.int32),
                   jax.ShapeDtypeStruct((T, ROUTER_W), jnp.float32)],
        compiler_params=_params("parallel"), name="ln_router",
    )(x, y, g.reshape(1, D), b.reshape(1, D), w_r2, b_r)


GATHER_UNROLL = 8
DMA_PRIORITIES = 2


def _row_copy(src_ref, src_row, dst_ref, r, sem):
    return pltpu.make_async_copy(src_ref.at[pl.ds(src_row, 1), :], dst_ref.at[pl.ds(r, 1), :], sem)


def _start_row_gather(idx_ref, base, src_ref, dst_ref, sem):
    def start(g, carry):
        for k in range(GATHER_UNROLL):
            r = g * GATHER_UNROLL + k
            _row_copy(src_ref, idx_ref[base + r], dst_ref, r, sem).start(priority=k % DMA_PRIORITIES)
        return carry

    lax.fori_loop(0, dst_ref.shape[0] // GATHER_UNROLL, start, 0)


def _wait_row_gather(src_ref, dst_ref, sem):
    def wait(r, carry):
        _row_copy(src_ref, 0, dst_ref, r, sem).wait()
        return carry

    lax.fori_loop(0, dst_ref.shape[0], wait, 0, unroll=GATHER_UNROLL)


def _gather_unpack_kernel(idx_ref, tv_ref, src_ref, o_ref, buf_ref, sem):
    i = pl.program_id(0)

    @pl.when(tv_ref[i] == 1)
    def _():
        _start_row_gather(idx_ref, i * o_ref.shape[0], src_ref, buf_ref, sem)
        _wait_row_gather(src_ref, buf_ref, sem)
        lo, hi = _unpack_bf16_halves(buf_ref[...])
        half = lo.shape[1]
        o_ref[:, :half] = lo
        o_ref[:, half:] = hi

    @pl.when(tv_ref[i] == 0)
    def _():
        o_ref[...] = jnp.zeros_like(o_ref)


def gather_expert_rows(src, idx, tile_v):
    R = idx.shape[0]
    D = src.shape[1]
    rows = MOE_TILE
    assert R % rows == 0 and rows % GATHER_UNROLL == 0
    return pl.pallas_call(
        _gather_unpack_kernel,
        grid_spec=pltpu.PrefetchScalarGridSpec(
            num_scalar_prefetch=2, grid=(R // rows,),
            in_specs=[pl.BlockSpec(memory_space=pl.ANY)],
            out_specs=pl.BlockSpec((rows, 2 * D), lambda i, idx, tv: (i, 0)),
            scratch_shapes=[pltpu.VMEM((rows, D), src.dtype), pltpu.SemaphoreType.DMA(())]),
        out_shape=jax.ShapeDtypeStruct((R, 2 * D), jnp.bfloat16),
        compiler_params=_params("arbitrary"), name="gather_rows",
    )(idx, tile_v, src)


def _expert_changed(te_ref, i):
    return (i == 0) | (te_ref[i] != te_ref[jnp.maximum(i - 1, 0)])


def _moe_up_kernel(te_ref, tv_ref, x_ref, wg_ref, wu_ref, h_ref, wgb_ref, wub_ref):
    i = pl.program_id(1)

    @pl.when(_expert_changed(te_ref, i))
    def _():
        wgb_ref[...] = wg_ref[0].astype(jnp.bfloat16)
        wub_ref[...] = wu_ref[0].astype(jnp.bfloat16)

    @pl.when(tv_ref[i] == 1)
    def _():
        x = x_ref[...]
        g = jnp.dot(x, wgb_ref[...], preferred_element_type=jnp.float32)
        u = jnp.dot(x, wub_ref[...], preferred_element_type=jnp.float32)
        h_ref[...] = (g * _sigmoid(g) * u).astype(h_ref.dtype)

    @pl.when(tv_ref[i] == 0)
    def _():
        h_ref[...] = jnp.zeros_like(h_ref)


def moe_up(xs, w_gate, w_up, tile_e, tile_v, tn=512):
    P = xs.shape[0]
    tm = MOE_TILE
    _, D, F = w_gate.shape
    w_spec = pl.BlockSpec((1, D, tn), lambda j, i, te, tv: (te[i], 0, j))
    return pl.pallas_call(
        _moe_up_kernel,
        grid_spec=pltpu.PrefetchScalarGridSpec(
            num_scalar_prefetch=2, grid=(F // tn, P // tm),
            in_specs=[pl.BlockSpec((tm, D), lambda j, i, te, tv: (i, 0)), w_spec, w_spec],
            out_specs=pl.BlockSpec((tm, tn), lambda j, i, te, tv: (i, j)),
            scratch_shapes=[pltpu.VMEM((D, tn), jnp.bfloat16), pltpu.VMEM((D, tn), jnp.bfloat16)]),
        out_shape=jax.ShapeDtypeStruct((P, F), jnp.bfloat16),
        compiler_params=_params("arbitrary", "arbitrary"), name="moe_up",
    )(tile_e, tile_v, xs, w_gate, w_up)


def _moe_down_kernel(te_ref, tv_ref, h_ref, wa_ref, wb_ref, y_ref, wab_ref, wbb_ref):
    i = pl.program_id(1)

    @pl.when(_expert_changed(te_ref, i))
    def _():
        wab_ref[...] = wa_ref[0].astype(jnp.bfloat16)
        wbb_ref[...] = wb_ref[0].astype(jnp.bfloat16)

    @pl.when(tv_ref[i] == 1)
    def _():
        h = h_ref[...]
        ya = jnp.dot(h, wab_ref[...], preferred_element_type=jnp.float32)
        yb = jnp.dot(h, wbb_ref[...], preferred_element_type=jnp.float32)
        y_ref[...] = _pack_bf16_pair(ya.astype(jnp.bfloat16), yb.astype(jnp.bfloat16))

    @pl.when(tv_ref[i] == 0)
    def _():
        y_ref[...] = jnp.zeros_like(y_ref)


def moe_down(h, w_down, tile_e, tile_v, tn=1024):
    P, F = h.shape
    tm = MOE_TILE
    half = w_down.shape[2] // 2
    nj = half // tn
    return pl.pallas_call(
        _moe_down_kernel,
        grid_spec=pltpu.PrefetchScalarGridSpec(
            num_scalar_prefetch=2, grid=(nj, P // tm),
            in_specs=[pl.BlockSpec((tm, F), lambda j, i, te, tv: (i, 0)),
                      pl.BlockSpec((1, F, tn), lambda j, i, te, tv: (te[i], 0, j)),
                      pl.BlockSpec((1, F, tn), lambda j, i, te, tv: (te[i], 0, j + nj))],
            out_specs=pl.BlockSpec((tm, tn), lambda j, i, te, tv: (i, j)),
            scratch_shapes=[pltpu.VMEM((F, tn), jnp.bfloat16), pltpu.VMEM((F, tn), jnp.bfloat16)]),
        out_shape=jax.ShapeDtypeStruct((P, half), jnp.uint32),
        compiler_params=_params("arbitrary", "arbitrary"), name="moe_down",
    )(tile_e, tile_v, h, w_down, w_down)


def _ln_out_kernel(pos_ref, x_ref, w_ref, g_ref, b_ref, y_ref, o_ref, buf_ref, sem):
    i = pl.program_id(0)
    n = pl.num_programs(0)
    tm = x_ref.shape[0]

    def parts(step, slot):
        return [(k * n * tm + step * tm, buf_ref.at[slot, pl.ds(k * tm, tm)]) for k in range(TOP_K)]

    def start(step, slot):
        for base, dst in parts(step, slot):
            _start_row_gather(pos_ref, base, y_ref, dst, sem.at[slot])

    @pl.when(i == 0)
    def _():
        start(0, 0)

    @pl.when(i + 1 < n)
    def _():
        start(i + 1, (i + 1) % 2)

    slot = i % 2
    for _, dst in parts(0, slot):
        _wait_row_gather(y_ref, dst, sem.at[slot])

    yb = buf_ref[slot]
    lo0, hi0 = _unpack_halves_f32(yb[:tm])
    lo1, hi1 = _unpack_halves_f32(yb[tm:])
    w = w_ref[...]
    w0, w1 = w[:, 0:1], w[:, 1:2]
    x = x_ref[...]
    half = lo0.shape[1]
    z = jnp.concatenate([DEEPNORM_ALPHA * x[:, :half] + (w0 * lo0 + w1 * lo1),
                         DEEPNORM_ALPHA * x[:, half:] + (w0 * hi0 + w1 * hi1)], axis=1)
    o_ref[...] = _ln_body(z, g_ref, b_ref)


def ln_out(x, y, pos, wts, g, b):
    T, D = x.shape
    tm = _tile(T, 256)
    row = pl.BlockSpec((tm, D), lambda i, pos: (i, 0))
    vec = pl.BlockSpec((1, D), lambda i, pos: (0, 0))
    return pl.pallas_call(
        _ln_out_kernel,
        grid_spec=pltpu.PrefetchScalarGridSpec(
            num_scalar_prefetch=1, grid=(T // tm,),
            in_specs=[row, pl.BlockSpec((tm, ROUTER_W), lambda i, pos: (i, 0)), vec, vec,
                      pl.BlockSpec(memory_space=pl.ANY)],
            out_specs=row,
            scratch_shapes=[pltpu.VMEM((2, TOP_K * tm, D // 2), jnp.uint32), pltpu.SemaphoreType.DMA((2,))]),
        out_shape=jax.ShapeDtypeStruct((T, D), jnp.float32),
        compiler_params=_params("arbitrary"), name="ln_out",
    )(pos, x, wts, g.reshape(1, D), b.reshape(1, D), y)


def _rot_cols(w):
    half = w.shape[-1] // 2
    return jnp.concatenate([-w[..., half:], w[..., :half]], axis=-1)


IN_OFFS = tuple(int(o) for o in np.cumsum((0,) + IN_SIZES))


def _prep_w_in_t(w_in):
    wt = w_in.T.astype(jnp.bfloat16)
    g, cq, ckv, kr = (wt[IN_OFFS[n]:IN_OFFS[n + 1]] for n in range(4, 8))
    kr_rot = _rot_cols(kr.T).T
    pad = jnp.zeros((SMALL_W - SMALL_GATES - g.shape[0], wt.shape[1]), wt.dtype)
    return wt, jnp.concatenate([cq, ckv, kr, kr_rot, g, pad], axis=0)


def _prep_w_uq(w_uq):
    w = w_uq.reshape(Q_LORA, A_HEADS, A_NOPE + A_ROPE)
    rope = w[:, :, A_NOPE:]
    w = jnp.concatenate([w[:, :, :A_NOPE], rope, _rot_cols(rope)], axis=-1)
    return w.reshape(Q_LORA, A_HEADS * Q_HEAD_W).astype(jnp.bfloat16)


def _rope_table(seq):
    inv = 1.0 / (ROPE_THETA ** (jnp.arange(0, A_ROPE, 2, dtype=jnp.float32) / A_ROPE))
    ang = jnp.arange(seq, dtype=jnp.float32)[:, None] * inv[None, :]
    cos, sin = jnp.cos(ang), jnp.sin(ang)
    return jnp.concatenate([cos, cos, sin, sin], axis=-1)


def _route_metadata(ids, T):
    tm = MOE_TILE
    n_tiles = (TOP_K * T) // tm + N_EXPERTS
    P = n_tiles * tm
    eid = jnp.concatenate([ids[:, 0], ids[:, 1]])
    tok = jnp.concatenate([jnp.arange(T, dtype=jnp.int32)] * TOP_K)
    onehot = (eid[:, None] == jnp.arange(N_EXPERTS, dtype=jnp.int32)[None, :]).astype(jnp.int32)
    rank = jnp.sum((jnp.cumsum(onehot, axis=0) - onehot) * onehot, axis=1)
    counts = jnp.sum(onehot, axis=0)
    padded = ((counts + tm - 1) // tm) * tm
    ends = jnp.cumsum(padded)
    starts = ends - padded
    pos = (starts[eid] + rank).astype(jnp.int32)
    src = (jnp.arange(P, dtype=jnp.int32) % T).at[pos].set(tok)
    tile_start = jnp.arange(n_tiles, dtype=jnp.int32) * tm
    tile_e = jnp.minimum(jnp.searchsorted(ends, tile_start, side="right"), N_EXPERTS - 1).astype(jnp.int32)
    tile_v = (tile_start < ends[-1]).astype(jnp.int32)
    return src, pos, tile_e, tile_v


def kernel(x, ln0_g, ln0_b, w_in, b_gates, conv_qk, head_norm_g, q_norm_g, w_uq, kv_norm_g, w_ukv, w_pm, w_pa,
           w_out, ln1_g, ln1_b, w_rg, b_rg, w_re, b_re, w_e_gate, w_e_up, w_e_down, ln2_g, ln2_b):
    B, S, D = x.shape
    T = B * S
    bf16 = jnp.bfloat16
    cs = _rope_table(S)
    xf, xb = layer_norm_in(x.reshape(T, D), ln0_g, ln0_b)
    for l in range(DEPTH):
        w_in_t, w_small_t = _prep_w_in_t(w_in[l])
        qk = matmul_nt(xb, w_in_t, bf16, [(IN_OFFS[0], 2 * M_QK)], name="proj_qk")
        v_o_g = matmul_nt(xb, w_in_t, bf16, [(IN_OFFS[2], 2 * M_V), (IN_OFFS[8], 2 * D_MODEL)], name="proj_vog")
        small = matmul_nt(xb, w_small_t, jnp.float32, [(0, SMALL_W)], name="proj_small")

        k_scale = jnp.concatenate([jnp.ones((1, M_QK), jnp.float32),
                                   jnp.full((1, M_QK), M_QK_DIM ** -0.5, jnp.float32)], axis=1)
        qk_c = conv_silu(qk, conv_qk[l], k_scale, S)
        nc = S // MLSTM_CHUNK
        gates = small[:, SMALL_GATES:SMALL_GATES + 4 * M_HEADS].reshape(B, nc, MLSTM_CHUNK, 4, M_HEADS)
        gates = gates.transpose(0, 4, 1, 3, 2)
        bias = b_gates[l].reshape(4, M_HEADS).T.reshape(M_HEADS, 4, 1)
        hm = mlstm(qk_c, v_o_g, gates, bias, head_norm_g[l], B, S)

        cqn, ckvn, krope = mla_prep(small, cs, q_norm_g[l], kv_norm_g[l], S)
        q_cat = q_up(cqn, _prep_w_uq(w_uq[l]), cs, S)
        kv = matmul(ckvn, w_ukv[l].astype(bf16), bf16, name="kv_up")
        att = attention(q_cat, kv, krope, B, S)

        merged = merge(hm, att, w_pm[l].astype(bf16), w_pa[l].astype(bf16), v_o_g)
        mixed = matmul(merged, w_out[l].astype(bf16), jnp.float32, name="proj_out")

        w_hi, w_mid = _split2(jnp.concatenate([w_rg[l], w_re[l]], axis=1))
        w_r2 = jnp.stack([jnp.pad(jnp.concatenate([w_hi, w_mid], axis=1), ((0, 0), (0, ROUTER_W - 2 * ROUTER_N))),
                          jnp.pad(w_hi, ((0, 0), (0, ROUTER_W - ROUTER_N)))])
        b_r = jnp.pad(jnp.concatenate([b_rg[l], b_re[l]]), (0, ROUTER_W - ROUTER_N))
        x1, x1p, ids, wts = ln_router(xf, mixed, ln1_g[l], ln1_b[l], w_r2, b_r.reshape(1, ROUTER_W))

        src, pos, tile_e, tile_v = _route_metadata(ids, T)
        xs = gather_expert_rows(x1p, src, tile_v)
        h = moe_up(xs, w_e_gate[l], w_e_up[l], tile_e, tile_v)
        y = moe_down(h, w_e_down[l], tile_e, tile_v)
        xf = ln_out(x1, y, pos, wts, ln2_g[l], ln2_b[l])
        if l + 1 < DEPTH:
            xb = xf.astype(bf16)
    return xf.reshape(B, S, D)
```
